```python
import jax, jax.numpy as jnp
from jax import lax
import numpy as np

D_MODEL = 1024
BATCH = 32
SEQ = 2048
DEPTH = 2

N_A_LAYERS = DEPTH // 2
N_B_LAYERS = DEPTH - N_A_LAYERS
PLE_DIM = 256
CONV_WIDTH = 31
CONV_EXPAND = 2
CONV_CH = CONV_EXPAND * D_MODEL
HEAD_DIM = 64
N_HEADS = D_MODEL // HEAD_DIM
ATTN_WIDTH = N_HEADS * HEAD_DIM
DILATION_GROUPS = ((128, 1), (512, 4), (2048, 16))
N_GROUPS = len(DILATION_GROUPS)
ROPE_THETA = 500000.0
ROPE_DIM = HEAD_DIM // 4
EPS = 1e-6
NEG_INF = -1e30

kernel_name = "yoco_conformer_dilated_hybrid"


def rmsnorm(x, g):
    xf = x.astype(jnp.float32)
    y = xf * lax.rsqrt(jnp.mean(xf * xf, axis=-1, keepdims=True) + EPS)
    return (y * g.astype(jnp.float32)).astype(x.dtype)


def layernorm(x, g, b):
    xf = x.astype(jnp.float32)
    mu = jnp.mean(xf, axis=-1, keepdims=True)
    var = jnp.mean(jnp.square(xf - mu), axis=-1, keepdims=True)
    y = (xf - mu) * lax.rsqrt(var + EPS) * g.astype(jnp.float32) + b.astype(jnp.float32)
    return y.astype(x.dtype)


def rope_partial(x, pos):
    half = ROPE_DIM // 2
    inv = ROPE_THETA ** (-jnp.arange(half, dtype=jnp.float32) * (2.0 / ROPE_DIM))
    ang = pos.astype(jnp.float32)[:, None] * inv[None, :]
    cos = jnp.cos(ang)[None, :, None, :]
    sin = jnp.sin(ang)[None, :, None, :]
    xf = x.astype(jnp.float32)
    x1 = xf[..., :half]
    x2 = xf[..., half:ROPE_DIM]
    out = jnp.concatenate([x1 * cos - x2 * sin, x1 * sin + x2 * cos, xf[..., ROPE_DIM:]], axis=-1)
    return out.astype(x.dtype)


def mixer_a(u, w_in, conv_w, conv_b, ln_g, ln_b, w_out):
    a, b, z = jnp.split(u @ w_in, 3, axis=-1)
    glu = a * jax.nn.sigmoid(b)
    y = lax.conv_general_dilated(
        glu, conv_w[:, None, :], window_strides=(1,),
        padding=[(CONV_WIDTH - 1, 0)],
        dimension_numbers=('NWC', 'WIO', 'NWC'),
        feature_group_count=CONV_CH)
    y = jax.nn.silu(layernorm(y + conv_b, ln_g, ln_b))
    return (y * jax.nn.silu(z)) @ w_out


def dilated_attention(q, k, v, span, dil):
    B, S, H, Dh = q.shape
    L = S // dil
    nb = -(-L // span)
    Lp = nb * span
    pad = Lp - L
    N = B * dil

    def to_res(t):
        return t.reshape(B, L, dil, H, Dh).transpose(0, 2, 1, 3, 4).reshape(N, L, H, Dh)

    qb = jnp.pad(to_res(q), ((0, 0), (0, pad), (0, 0), (0, 0))).reshape(N, nb, span, H, Dh)
    kp = jnp.pad(to_res(k), ((0, 0), (span, pad), (0, 0), (0, 0)))
    vp = jnp.pad(to_res(v), ((0, 0), (span, pad), (0, 0), (0, 0)))

    def band(t):
        return jnp.concatenate([t[:, :Lp].reshape(N, nb, span, H, Dh),
                                t[:, span:].reshape(N, nb, span, H, Dh)], axis=2)

    kb, vb = band(kp), band(vp)
    s = jnp.einsum('nbqhd,nbkhd->nbhqk', qb, kb).astype(jnp.float32) * (Dh ** -0.5)
    qi = jnp.arange(span)[:, None]
    kj = jnp.arange(2 * span)[None, :]
    dist = qi - kj + span
    key_idx = jnp.arange(nb)[:, None] * span + jnp.arange(2 * span)[None, :] - span
    mask = ((dist >= 0) & (dist <= span))[None] & (key_idx >= 0)[:, None, :]
    s = jnp.where(mask[None, :, None], s, NEG_INF)
    lse = jax.nn.logsumexp(s, axis=-1)
    prob = jnp.exp(s - lse[..., None])
    o = jnp.einsum('nbhqk,nbkhd->nbqhd', prob.astype(v.dtype), vb)
    o = o.reshape(B, dil, Lp, H, Dh)[:, :, :L].transpose(0, 2, 1, 3, 4).reshape(B, S, H, Dh)
    lse = lse.transpose(0, 1, 3, 2).reshape(B, dil, Lp, H)[:, :, :L]
    lse = lse.transpose(0, 2, 1, 3).reshape(B, S, H)
    return o, lse


def mixer_b(u, w_in, q_g, w_out, k_sh, v_sh, pos):
    B, S, _ = u.shape
    proj = u @ w_in
    q_all = proj[..., :N_GROUPS * ATTN_WIDTH].reshape(B, S, N_GROUPS, N_HEADS, HEAD_DIM)
    gate = proj[..., N_GROUPS * ATTN_WIDTH:]
    outs, lses = [], []
    for g, (window, dil) in enumerate(DILATION_GROUPS):
        q = rope_partial(rmsnorm(q_all[:, :, g], q_g[g]), pos)
        o, lse = dilated_attention(q, k_sh, v_sh, window // dil, dil)
        outs.append(o)
        lses.append(lse)
    wts = jax.nn.softmax(jnp.stack(lses, axis=0), axis=0)
    o = jnp.einsum('gbsh,gbshd->bshd', wts.astype(outs[0].dtype), jnp.stack(outs, axis=0))
    return (o.reshape(B, S, ATTN_WIDTH) * jax.nn.silu(gate)) @ w_out


def setup_inputs(seed: int = 0) -> dict:
    key = jax.random.key(seed)
    ks = jax.random.split(key, 20)
    f32 = jnp.float32

    def nrm(k, shape, fan_in):
        return jax.random.normal(k, shape, f32) * (fan_in ** -0.5)

    def gain(k, shape):
        return 1.0 + 0.02 * jax.random.normal(k, shape, f32)

    return {
        "x": jax.random.normal(ks[0], (BATCH, SEQ, D_MODEL), f32),
        "p": jax.random.normal(ks[1], (DEPTH, BATCH, SEQ, PLE_DIM), f32),
        "norm_g": gain(ks[2], (DEPTH, D_MODEL)),
        "w_in_a": nrm(ks[3], (N_A_LAYERS, D_MODEL, 3 * CONV_CH), D_MODEL),
        "conv_w": nrm(ks[4], (N_A_LAYERS, CONV_WIDTH, CONV_CH), CONV_WIDTH),
        "conv_b": 0.02 * jax.random.normal(ks[5], (N_A_LAYERS, CONV_CH), f32),
        "ln_g": gain(ks[6], (N_A_LAYERS, CONV_CH)),
        "ln_b": 0.02 * jax.random.normal(ks[7], (N_A_LAYERS, CONV_CH), f32),
        "w_out_a": nrm(ks[8], (N_A_LAYERS, CONV_CH, D_MODEL), CONV_CH),
        "kv_norm_g": gain(ks[9], (D_MODEL,)),
        "w_kv": nrm(ks[10], (D_MODEL, 2 * ATTN_WIDTH), D_MODEL),
        "k_norm_g": gain(ks[11], (HEAD_DIM,)),
        "w_in_b": nrm(ks[12], (N_B_LAYERS, D_MODEL, (N_GROUPS + 1) * ATTN_WIDTH), D_MODEL),
        "q_norm_g": gain(ks[13], (N_B_LAYERS, N_GROUPS, HEAD_DIM)),
        "w_out_b": nrm(ks[14], (N_B_LAYERS, ATTN_WIDTH, D_MODEL), ATTN_WIDTH),
        "ple_norm_g": gain(ks[15], (DEPTH, D_MODEL)),
        "w_ple_gate": nrm(ks[16], (DEPTH, D_MODEL, D_MODEL), D_MODEL),
        "w_ple_proj": nrm(ks[17], (DEPTH, PLE_DIM, D_MODEL), PLE_DIM),
    }


def reference(x, p, norm_g, w_in_a, conv_w, conv_b, ln_g, ln_b, w_out_a,
              kv_norm_g, w_kv, k_norm_g, w_in_b, q_norm_g, w_out_b,
              ple_norm_g, w_ple_gate, w_ple_proj):
    B, S, _ = x.shape
    pos = jnp.arange(S)
    k_sh = None
    v_sh = None
    for i in range(DEPTH):
        u = rmsnorm(x, norm_g[i])
        if i < N_A_LAYERS:
            j = i
            h = x + mixer_a(u, w_in_a[j], conv_w[j], conv_b[j], ln_g[j], ln_b[j], w_out_a[j])
        else:
            j = i - N_A_LAYERS
            h = x + mixer_b(u, w_in_b[j], q_norm_g[j], w_out_b[j], k_sh, v_sh, pos)
        ple_gate = jax.nn.sigmoid(rmsnorm(h, ple_norm_g[i]) @ w_ple_gate[i])
        h = h + ple_gate * (p[i] @ w_ple_proj[i])
        if i == N_A_LAYERS - 1:
            k_sh, v_sh = jnp.split(rmsnorm(h, kv_norm_g) @ w_kv, 2, axis=-1)
            k_sh = rope_partial(rmsnorm(k_sh.reshape(B, S, N_HEADS, HEAD_DIM), k_norm_g), pos)
            v_sh = v_sh.reshape(B, S, N_HEADS, HEAD_DIM)
        x = h
    return x
```

```python
import functools

import jax
import jax.numpy as jnp
from jax import lax
from jax.experimental import pallas as pl
from jax.experimental.pallas import tpu as pltpu

D_MODEL = 1024
SEQ = 2048
PLE_DIM = 256
CONV_WIDTH = 31
CONV_CH = 2048
HEAD_DIM = 64
N_HEADS = 16
ATTN_WIDTH = 1024
DILATION_GROUPS = ((128, 1), (512, 4), (2048, 16))
N_GROUPS = 3
ROPE_THETA = 500000.0
ROPE_DIM = 16
EPS = 1e-6
NEG_INF = -1e30

LANES = 128
HEAD_GROUP = 4
HG_W = HEAD_GROUP * HEAD_DIM
N_HG = N_HEADS // HEAD_GROUP
SPAN = 128
TM = 256
HALO = 32
CONV_ROWS = 64
N_SLAB = CONV_CH // LANES
IN_CHUNK = 512
VMEM_LIMIT = 56 * 1024 * 1024

F32 = jnp.float32
BF16 = jnp.bfloat16


def _const_spec(shape):
    nd = len(shape)
    return pl.BlockSpec(shape, lambda *_: (0,) * nd, pipeline_mode=pl.Buffered(1))


def _rmsnorm(x, g):
    ms = jnp.sum(x * x, axis=-1, keepdims=True) * (1.0 / x.shape[-1])
    return x * lax.rsqrt(ms + EPS) * g


def _sigmoid(x):
    return 1.0 / (1.0 + jnp.exp(-x))


def _head_sumsq(x, bd):
    sq = x * x
    hi = sq.astype(BF16)
    lo = (sq - hi.astype(F32)).astype(BF16)
    return (jnp.dot(hi, bd, preferred_element_type=F32)
            + jnp.dot(lo, bd, preferred_element_type=F32))


def _rope(x, cos, sa, sb):
    return (x * cos + pltpu.roll(x, LANES - ROPE_DIM // 2, axis=1) * sa
            + pltpu.roll(x, ROPE_DIM // 2, axis=1) * sb)


def _head_norm_rope(x, bd, g, cos, sa, sb, scale):
    ss = _head_sumsq(x, bd)
    xn = x * lax.rsqrt(ss * (1.0 / HEAD_DIM) + EPS) * g
    halves = [_rope(xn[:, i * LANES:(i + 1) * LANES], cos, sa, sb) for i in range(HG_W // LANES)]
    out = jnp.concatenate(halves, axis=1)
    if scale != 1.0:
        out = out * scale
    return out.astype(BF16)


def _ple(hmid, p, pg, wpg, wpp):
    gate = _sigmoid(jnp.dot(_rmsnorm(hmid, pg).astype(BF16), wpg, preferred_element_type=F32))
    pe = jnp.dot(p.astype(BF16), wpp, preferred_element_type=F32)
    return hmid + gate * pe


def _layer0_kernel(x_ref, p_ref, ng_ref, win_ref, cw_ref, cb_ref, lng_ref, lnb_ref, wout_ref,
                   pg_ref, wpg_ref, wpp_ref, kvg_ref, wkv_ref, kg_ref, bd_ref,
                   cos_ref, sa_ref, sb_ref,
                   h_ref, k_ref, v_ref,
                   gbuf, ybuf, zbuf, mbuf):
    t = pl.program_id(1)

    @pl.when(t == 0)
    def _():
        gbuf[:, 0:HALO, :] = jnp.zeros((N_SLAB, HALO, LANES), F32)

    x = x_ref[...]
    u = _rmsnorm(x, ng_ref[...]).astype(BF16)

    for c in range(CONV_CH // IN_CHUNK):
        lo = c * IN_CHUNK
        a = jnp.dot(u, win_ref[:, lo:lo + IN_CHUNK], preferred_element_type=F32)
        b = jnp.dot(u, win_ref[:, CONV_CH + lo:CONV_CH + lo + IN_CHUNK], preferred_element_type=F32)
        z = jnp.dot(u, win_ref[:, 2 * CONV_CH + lo:2 * CONV_CH + lo + IN_CHUNK],
                    preferred_element_type=F32)
        glu = a * _sigmoid(b)
        sz = z * _sigmoid(z)
        for s in range(IN_CHUNK // LANES):
            slab = c * (IN_CHUNK // LANES) + s
            gbuf[slab, HALO:HALO + TM, :] = glu[:, s * LANES:(s + 1) * LANES]
            zbuf[slab] = sz[:, s * LANES:(s + 1) * LANES]

    tap0 = HALO - (CONV_WIDTH - 1)

    def conv_slab(s, carry):
        w = cw_ref[s]
        bias = cb_ref[s]
        for rb in range(TM // CONV_ROWS):
            acc = jnp.zeros((CONV_ROWS, LANES), F32)
            for k in range(CONV_WIDTH):
                acc = acc + gbuf[s, pl.ds(rb * CONV_ROWS + tap0 + k, CONV_ROWS), :] * w[k:k + 1, :]
            ybuf[s, pl.ds(rb * CONV_ROWS, CONV_ROWS), :] = acc + bias
        return carry

    lax.fori_loop(0, N_SLAB, conv_slab, 0)

    gbuf[:, 0:HALO, :] = gbuf[:, TM:TM + HALO, :]

    tot = ybuf[0]
    for s in range(1, N_SLAB):
        tot = tot + ybuf[s]
    mu = jnp.sum(tot, axis=-1, keepdims=True) * (1.0 / CONV_CH)
    d0 = ybuf[0] - mu
    vs = d0 * d0
    for s in range(1, N_SLAB):
        ds = ybuf[s] - mu
        vs = vs + ds * ds
    rstd = lax.rsqrt(jnp.sum(vs, axis=-1, keepdims=True) * (1.0 / CONV_CH) + EPS)
    for s in range(N_SLAB):
        yn = (ybuf[s] - mu) * rstd * lng_ref[s] + lnb_ref[s]
        act = yn * _sigmoid(yn)
        mbuf[:, s * LANES:(s + 1) * LANES] = (act * zbuf[s]).astype(BF16)
    hmid = x + jnp.dot(mbuf[...], wout_ref[...], preferred_element_type=F32)

    h = _ple(hmid, p_ref[...], pg_ref[...], wpg_ref[...], wpp_ref[...])
    h_ref[...] = h

    kv = jnp.dot(_rmsnorm(h, kvg_ref[...]).astype(BF16), wkv_ref[...], preferred_element_type=F32)
    bd = bd_ref[...]
    cos, sa, sb = cos_ref[...], sa_ref[...], sb_ref[...]
    for hg in range(N_HG):
        kh = kv[:, hg * HG_W:(hg + 1) * HG_W]
        k_ref[hg] = _head_norm_rope(kh, bd, kg_ref[...], cos, sa, sb, 1.0)
        v_ref[hg] = kv[:, ATTN_WIDTH + hg * HG_W:ATTN_WIDTH + (hg + 1) * HG_W].astype(BF16)


def _layer0(x, p, ng, win, cw, cb, lng, lnb, wout, pg, wpg, wpp, kvg, wkv, kg, bd, cos, sa, sb):
    B, S, D = x.shape
    nt = S // TM
    row = lambda b, t: (b, t, 0)
    hg_out = pl.BlockSpec((None, N_HG, TM, HG_W), lambda b, t: (b, 0, t, 0))
    tab = pl.BlockSpec((TM, LANES), lambda b, t: (t, 0))
    return pl.pallas_call(
        _layer0_kernel,
        grid=(B, nt),
        in_specs=[
            pl.BlockSpec((None, TM, D), row),
            pl.BlockSpec((None, None, TM, PLE_DIM), lambda b, t: (0, b, t, 0)),
            _const_spec(ng.shape), _const_spec(win.shape), _const_spec(cw.shape),
            _const_spec(cb.shape), _const_spec(lng.shape), _const_spec(lnb.shape),
            _const_spec(wout.shape), _const_spec(pg.shape), _const_spec(wpg.shape),
            _const_spec(wpp.shape), _const_spec(kvg.shape), _const_spec(wkv.shape),
            _const_spec(kg.shape), _const_spec(bd.shape),
            tab, tab, tab,
        ],
        out_specs=[pl.BlockSpec((None, TM, D), row), hg_out, hg_out],
        out_shape=[
            jax.ShapeDtypeStruct((B, S, D), F32),
            jax.ShapeDtypeStruct((B, N_HG, S, HG_W), BF16),
            jax.ShapeDtypeStruct((B, N_HG, S, HG_W), BF16),
        ],
        scratch_shapes=[
            pltpu.VMEM((N_SLAB, HALO + TM, LANES), F32),
            pltpu.VMEM((N_SLAB, TM, LANES), F32),
            pltpu.VMEM((N_SLAB, TM, LANES), F32),
            pltpu.VMEM((TM, CONV_CH), BF16),
        ],
        compiler_params=pltpu.CompilerParams(
            dimension_semantics=("arbitrary", "arbitrary"), vmem_limit_bytes=VMEM_LIMIT),
        name="layer0",
    )(x, p, ng, win, cw, cb, lng, lnb, wout, pg, wpg, wpp, kvg, wkv, kg, bd, cos, sa, sb)


def _qproj_kernel(h_ref, ng_ref, win_ref, qg_ref, bd_ref, cos_ref, sa_ref, sb_ref,
                  q1_ref, q2_ref, q3_ref, gate_ref):
    u = _rmsnorm(h_ref[...], ng_ref[...]).astype(BF16)
    bd = bd_ref[...]
    cos, sa, sb = cos_ref[...], sa_ref[...], sb_ref[...]
    for g, q_ref in enumerate((q1_ref, q2_ref, q3_ref)):
        q = jnp.dot(u, win_ref[:, g * ATTN_WIDTH:(g + 1) * ATTN_WIDTH], preferred_element_type=F32)
        for hg in range(N_HG):
            q_ref[hg] = _head_norm_rope(q[:, hg * HG_W:(hg + 1) * HG_W], bd, qg_ref[g:g + 1, :],
                                        cos, sa, sb, HEAD_DIM ** -0.5)
    z = jnp.dot(u, win_ref[:, N_GROUPS * ATTN_WIDTH:], preferred_element_type=F32)
    gate_ref[...] = (z * _sigmoid(z)).astype(BF16)


def _qproj(h, ng, win, qg, bd, cos, sa, sb):
    B, S, D = h.shape
    row = lambda b, t: (b, t, 0)
    hg_out = pl.BlockSpec((None, N_HG, TM, HG_W), lambda b, t: (b, 0, t, 0))
    tab = pl.BlockSpec((TM, LANES), lambda b, t: (t, 0))
    qshape = jax.ShapeDtypeStruct((B, N_HG, S, HG_W), BF16)
    return pl.pallas_call(
        _qproj_kernel,
        grid=(B, S // TM),
        in_specs=[pl.BlockSpec((None, TM, D), row), _const_spec(ng.shape), _const_spec(win.shape),
                  _const_spec(qg.shape), _const_spec(bd.shape), tab, tab, tab],
        out_specs=[hg_out, hg_out, hg_out, pl.BlockSpec((None, TM, ATTN_WIDTH), row)],
        out_shape=[qshape, qshape, qshape, jax.ShapeDtypeStruct((B, S, ATTN_WIDTH), BF16)],
        compiler_params=pltpu.CompilerParams(
            dimension_semantics=("arbitrary", "arbitrary"), vmem_limit_bytes=VMEM_LIMIT),
        name="qproj",
    )(h, ng, win, qg, bd, cos, sa, sb)


def _attn_kernel(q_ref, k_ref, v_ref, o_ref, lse_ref, *, dil, nb):
    lane = lax.broadcasted_iota(jnp.int32, (1, HG_W), 1)
    head_mask = [(lane >= h * HEAD_DIM) & (lane < (h + 1) * HEAD_DIM) for h in range(HEAD_GROUP)]
    row = lax.broadcasted_iota(jnp.int32, (SPAN, 2 * SPAN), 0)
    col = lax.broadcasted_iota(jnp.int32, (SPAN, 2 * SPAN), 1)
    band = (col >= row) & (col <= row + SPAN)
    tri = (lax.broadcasted_iota(jnp.int32, (SPAN, SPAN), 1)
           <= lax.broadcasted_iota(jnp.int32, (SPAN, SPAN), 0))

    def select_heads(vals):
        out = vals[HEAD_GROUP - 1]
        for h in range(HEAD_GROUP - 2, -1, -1):
            out = jnp.where(head_mask[h], vals[h], out)
        return out

    for r in range(dil):
        cs = slice(r * HG_W, (r + 1) * HG_W)
        for j in range(nb):
            rows = slice(j * SPAN, (j + 1) * SPAN)
            q = q_ref[rows, cs]
            if j == 0:
                kk, vv, valid = k_ref[rows, cs], v_ref[rows, cs], tri
            else:
                both = slice((j - 1) * SPAN, (j + 1) * SPAN)
                kk, vv, valid = k_ref[both, cs], v_ref[both, cs], band
            qs = jnp.concatenate(
                [jnp.where(head_mask[h], q, jnp.zeros_like(q)) for h in range(HEAD_GROUP)], axis=0)
            s = lax.dot_general(qs, kk, (((1,), (1,)), ((), ())), preferred_element_type=F32)
            ps, ms, ls = [], [], []
            for h in range(HEAD_GROUP):
                sh = jnp.where(valid, s[h * SPAN:(h + 1) * SPAN], NEG_INF)
                m = jnp.max(sh, axis=-1, keepdims=True)
                p = jnp.exp(sh - m)
                ls.append(jnp.sum(p, axis=-1, keepdims=True))
                ms.append(m)
                ps.append(p.astype(BF16))
            of = jnp.dot(jnp.concatenate(ps, axis=0), vv, preferred_element_type=F32)
            o = select_heads([of[h * SPAN:(h + 1) * SPAN] for h in range(HEAD_GROUP)])
            l = select_heads([jnp.broadcast_to(x, (SPAN, HG_W)) for x in ls])
            m = select_heads([jnp.broadcast_to(x, (SPAN, HG_W)) for x in ms])
            o_ref[rows, cs] = (o / l).astype(BF16)
            lse_ref[rows, cs] = m + jnp.log(l)


def _attention(q, k, v, dil):
    B, nhg, S, _ = q.shape
    L = S // dil
    nb = L // SPAN
    view = lambda a: a.reshape(B, nhg, L, dil * HG_W)
    spec = pl.BlockSpec((None, None, L, dil * HG_W), lambda b, g: (b, g, 0, 0))
    o, lse = pl.pallas_call(
        functools.partial(_attn_kernel, dil=dil, nb=nb),
        grid=(B, nhg),
        in_specs=[spec, spec, spec],
        out_specs=[spec, spec],
        out_shape=[jax.ShapeDtypeStruct((B, nhg, L, dil * HG_W), BF16),
                   jax.ShapeDtypeStruct((B, nhg, L, dil * HG_W), F32)],
        compiler_params=pltpu.CompilerParams(
            dimension_semantics=("arbitrary", "arbitrary"), vmem_limit_bytes=VMEM_LIMIT),
        name=f"attn_d{dil}",
    )(view(q), view(k), view(v))
    return o.reshape(B, nhg, S, HG_W), lse.reshape(B, nhg, S, HG_W)


def _layer1_out_kernel(o1_ref, o2_ref, o3_ref, l1_ref, l2_ref, l3_ref, gate_ref, h_ref, p_ref,
                       wout_ref, pg_ref, wpg_ref, wpp_ref, out_ref):
    cols = []
    for hg in range(N_HG):
        l1, l2, l3 = l1_ref[hg], l2_ref[hg], l3_ref[hg]
        m = jnp.maximum(jnp.maximum(l1, l2), l3)
        e1, e2, e3 = jnp.exp(l1 - m), jnp.exp(l2 - m), jnp.exp(l3 - m)
        num = (e1 * o1_ref[hg].astype(F32) + e2 * o2_ref[hg].astype(F32)
               + e3 * o3_ref[hg].astype(F32))
        cols.append(num / (e1 + e2 + e3))
    o = jnp.concatenate(cols, axis=1)
    y = jnp.dot((o * gate_ref[...].astype(F32)).astype(BF16), wout_ref[...],
                preferred_element_type=F32)
    hmid = h_ref[...] + y
    out_ref[...] = _ple(hmid, p_ref[...], pg_ref[...], wpg_ref[...], wpp_ref[...])


def _layer1_out(os, lses, gate, h, p, wout, pg, wpg, wpp):
    B, S, D = h.shape
    row = lambda b, t: (b, t, 0)
    hg_in = pl.BlockSpec((None, N_HG, TM, HG_W), lambda b, t: (b, 0, t, 0))
    return pl.pallas_call(
        _layer1_out_kernel,
        grid=(B, S // TM),
        in_specs=[hg_in] * 6 + [
            pl.BlockSpec((None, TM, ATTN_WIDTH), row),
            pl.BlockSpec((None, TM, D), row),
            pl.BlockSpec((None, None, TM, PLE_DIM), lambda b, t: (1, b, t, 0)),
            _const_spec(wout.shape), _const_spec(pg.shape), _const_spec(wpg.shape),
            _const_spec(wpp.shape)],
        out_specs=pl.BlockSpec((None, TM, D), row),
        out_shape=jax.ShapeDtypeStruct((B, S, D), F32),
        compiler_params=pltpu.CompilerParams(
            dimension_semantics=("arbitrary", "arbitrary"), vmem_limit_bytes=VMEM_LIMIT),
        name="layer1_out",
    )(*os, *lses, gate, h, p, wout, pg, wpg, wpp)


def _rope_tables(seq):
    half = ROPE_DIM // 2
    inv = ROPE_THETA ** (-jnp.arange(half, dtype=F32) * (2.0 / ROPE_DIM))
    ang = jnp.arange(seq).astype(F32)[:, None] * inv[None, :]
    cos, sin = jnp.cos(ang), jnp.sin(ang)
    ones = jnp.ones((seq, HEAD_DIM - ROPE_DIM), F32)
    zeros = jnp.zeros((seq, HEAD_DIM - ROPE_DIM), F32)
    zh = jnp.zeros((seq, half), F32)
    per_head = lambda parts: jnp.tile(jnp.concatenate(parts, axis=1), (1, LANES // HEAD_DIM))
    return (per_head([cos, cos, ones]), per_head([-sin, zh, zeros]), per_head([zh, sin, zeros]))


def kernel(x, p, norm_g, w_in_a, conv_w, conv_b, ln_g, ln_b, w_out_a, kv_norm_g, w_kv, k_norm_g,
           w_in_b, q_norm_g, w_out_b, ple_norm_g, w_ple_gate, w_ple_proj):
    B, S, D = x.shape
    assert (S, D) == (SEQ, D_MODEL) and S % TM == 0
    cos, sa, sb = _rope_tables(S)
    head_id = jnp.arange(HG_W) // HEAD_DIM
    bd = (head_id[:, None] == head_id[None, :]).astype(BF16)
    row = lambda a: a.reshape(1, -1)
    slabs = lambda a: a.reshape(N_SLAB, 1, LANES)
    cw = jnp.pad(conv_w[0], ((0, HALO - CONV_WIDTH), (0, 0)))
    cw = cw.reshape(HALO, N_SLAB, LANES).transpose(1, 0, 2)

    h1, k, v = _layer0(
        x, p, row(norm_g[0]), w_in_a[0].astype(BF16), cw, slabs(conv_b[0]), slabs(ln_g[0]),
        slabs(ln_b[0]), w_out_a[0].astype(BF16), row(ple_norm_g[0]), w_ple_gate[0].astype(BF16),
        w_ple_proj[0].astype(BF16), row(kv_norm_g), w_kv.astype(BF16),
        row(jnp.tile(k_norm_g, HEAD_GROUP)), bd, cos, sa, sb)

    q1, q2, q3, gate = _qproj(h1, row(norm_g[1]), w_in_b[0].astype(BF16),
                              jnp.tile(q_norm_g[0], (1, HEAD_GROUP)), bd, cos, sa, sb)
    os, lses = [], []
    for q, (_, dil) in zip((q1, q2, q3), DILATION_GROUPS):
        o, lse = _attention(q, k, v, dil)
        os.append(o)
        lses.append(lse)
    return _layer1_out(os, lses, gate, h1, p, w_out_b[0].astype(BF16), row(ple_norm_g[1]),
                       w_ple_gate[1].astype(BF16), w_ple_proj[1].astype(BF16))
```

```python
import functools

import jax
import jax.numpy as jnp
from jax import lax
from jax.experimental import pallas as pl
from jax.experimental.pallas import tpu as pltpu

D_MODEL = 1024
SEQ = 2048
PLE_DIM = 256
CONV_WIDTH = 31
CONV_CH = 2048
HEAD_DIM = 64
N_HEADS = 16
ATTN_WIDTH = 1024
DILATION_GROUPS = ((128, 1), (512, 4), (2048, 16))
N_GROUPS = 3
ROPE_THETA = 500000.0
ROPE_DIM = 16
EPS = 1e-6
NEG_INF = -1e30

LANES = 128
HEAD_GROUP = 4
HG_W = HEAD_GROUP * HEAD_DIM
N_HG = N_HEADS // HEAD_GROUP
SPAN = 128
TM = 256
HALO = 32
CONV_ROWS = 64
N_SLAB = CONV_CH // LANES
IN_CHUNK = 512
VMEM_LIMIT = 56 * 1024 * 1024

F32 = jnp.float32
BF16 = jnp.bfloat16


def _const_spec(shape):
    nd = len(shape)
    return pl.BlockSpec(shape, lambda *_: (0,) * nd, pipeline_mode=pl.Buffered(1))


def _rmsnorm(x, g):
    ms = jnp.sum(x * x, axis=-1, keepdims=True) * (1.0 / x.shape[-1])
    return x * lax.rsqrt(ms + EPS) * g


def _sigmoid(x):
    return 1.0 / (1.0 + jnp.exp(-x))


def _head_sumsq(x, bd):
    sq = x * x
    hi = sq.astype(BF16)
    lo = (sq - hi.astype(F32)).astype(BF16)
    return (jnp.dot(hi, bd, preferred_element_type=F32)
            + jnp.dot(lo, bd, preferred_element_type=F32))


def _rope(x, cos, sa, sb):
    return (x * cos + pltpu.roll(x, LANES - ROPE_DIM // 2, axis=1) * sa
            + pltpu.roll(x, ROPE_DIM // 2, axis=1) * sb)


def _head_norm_rope(x, bd, g, cos, sa, sb, scale):
    ss = _head_sumsq(x, bd)
    xn = x * lax.rsqrt(ss * (1.0 / HEAD_DIM) + EPS) * g
    halves = [_rope(xn[:, i * LANES:(i + 1) * LANES], cos, sa, sb) for i in range(HG_W // LANES)]
    out = jnp.concatenate(halves, axis=1)
    return out if scale == 1.0 else out * scale


def _view_spec(dil):
    return pl.BlockSpec((None, N_HG, TM // dil, dil * HG_W), lambda b, t: (b, 0, t, 0))


def _view_shape(batch, dil, dtype):
    return jax.ShapeDtypeStruct((batch, N_HG, SEQ // dil, dil * HG_W), dtype)


def _store_views(x, hg, scr, slab0, refs):
    halves = HG_W // LANES
    if any(dil > 1 for dil in refs):
        for s in range(halves):
            scr[slab0 + s] = x[:, s * LANES:(s + 1) * LANES]
    for dil, ref in refs.items():
        if dil == 1:
            ref[hg] = x.astype(BF16)
            continue
        for r in range(dil):
            for s in range(halves):
                piece = scr[slab0 + s, pl.ds(r, TM // dil, stride=dil), :]
                ref[hg, :, r * HG_W + s * LANES:r * HG_W + (s + 1) * LANES] = piece.astype(BF16)


def _load_view(ref, hg, dil, scr, slab0):
    if dil == 1:
        return ref[hg].astype(F32)
    halves = HG_W // LANES
    for r in range(dil):
        for s in range(halves):
            piece = ref[hg, :, r * HG_W + s * LANES:r * HG_W + (s + 1) * LANES]
            scr[slab0 + s, pl.ds(r, TM // dil, stride=dil), :] = piece.astype(F32)
    return jnp.concatenate([scr[slab0 + s] for s in range(halves)], axis=1)


def _ple(hmid, p, pg, wpg, wpp):
    gate = _sigmoid(jnp.dot(_rmsnorm(hmid, pg).astype(BF16), wpg, preferred_element_type=F32))
    pe = jnp.dot(p.astype(BF16), wpp, preferred_element_type=F32)
    return hmid + gate * pe


def _layer0_kernel(x_ref, p_ref, ng_ref, win_ref, cw_ref, cb_ref, lng_ref, lnb_ref, wout_ref,
                   pg_ref, wpg_ref, wpp_ref, kvg_ref, wkv_ref, kg_ref, bd_ref,
                   cos_ref, sa_ref, sb_ref,
                   h_ref, k1_ref, v1_ref, k4_ref, v4_ref, k16_ref, v16_ref,
                   gbuf, ybuf, zbuf, mbuf, vbuf):
    t = pl.program_id(1)

    @pl.when(t == 0)
    def _():
        gbuf[:, 0:HALO, :] = jnp.zeros((N_SLAB, HALO, LANES), F32)

    x = x_ref[...]
    u = _rmsnorm(x, ng_ref[...]).astype(BF16)

    for c in range(CONV_CH // IN_CHUNK):
        lo = c * IN_CHUNK
        a = jnp.dot(u, win_ref[:, lo:lo + IN_CHUNK], preferred_element_type=F32)
        b = jnp.dot(u, win_ref[:, CONV_CH + lo:CONV_CH + lo + IN_CHUNK], preferred_element_type=F32)
        z = jnp.dot(u, win_ref[:, 2 * CONV_CH + lo:2 * CONV_CH + lo + IN_CHUNK],
                    preferred_element_type=F32)
        glu = a * _sigmoid(b)
        sz = z * _sigmoid(z)
        for s in range(IN_CHUNK // LANES):
            slab = c * (IN_CHUNK // LANES) + s
            gbuf[slab, HALO:HALO + TM, :] = glu[:, s * LANES:(s + 1) * LANES]
            zbuf[slab] = sz[:, s * LANES:(s + 1) * LANES]

    tap0 = HALO - (CONV_WIDTH - 1)

    def conv_slab(s, carry):
        w = cw_ref[s]
        bias = cb_ref[s]
        for rb in range(TM // CONV_ROWS):
            acc = jnp.zeros((CONV_ROWS, LANES), F32)
            for k in range(CONV_WIDTH):
                acc = acc + gbuf[s, pl.ds(rb * CONV_ROWS + tap0 + k, CONV_ROWS), :] * w[k:k + 1, :]
            ybuf[s, pl.ds(rb * CONV_ROWS, CONV_ROWS), :] = acc + bias
        return carry

    lax.fori_loop(0, N_SLAB, conv_slab, 0)

    gbuf[:, 0:HALO, :] = gbuf[:, TM:TM + HALO, :]

    tot = ybuf[0]
    for s in range(1, N_SLAB):
        tot = tot + ybuf[s]
    mu = jnp.sum(tot, axis=-1, keepdims=True) * (1.0 / CONV_CH)
    d0 = ybuf[0] - mu
    vs = d0 * d0
    for s in range(1, N_SLAB):
        ds = ybuf[s] - mu
        vs = vs + ds * ds
    rstd = lax.rsqrt(jnp.sum(vs, axis=-1, keepdims=True) * (1.0 / CONV_CH) + EPS)
    for s in range(N_SLAB):
        yn = (ybuf[s] - mu) * rstd * lng_ref[s] + lnb_ref[s]
        act = yn * _sigmoid(yn)
        mbuf[:, s * LANES:(s + 1) * LANES] = (act * zbuf[s]).astype(BF16)
    hmid = x + jnp.dot(mbuf[...], wout_ref[...], preferred_element_type=F32)

    h = _ple(hmid, p_ref[...], pg_ref[...], wpg_ref[...], wpp_ref[...])
    h_ref[...] = h

    kv = jnp.dot(_rmsnorm(h, kvg_ref[...]).astype(BF16), wkv_ref[...], preferred_element_type=F32)
    bd = bd_ref[...]
    cos, sa, sb = cos_ref[...], sa_ref[...], sb_ref[...]
    for hg in range(N_HG):
        kh = kv[:, hg * HG_W:(hg + 1) * HG_W]
        kh = _head_norm_rope(kh, bd, kg_ref[...], cos, sa, sb, 1.0)
        vh = kv[:, ATTN_WIDTH + hg * HG_W:ATTN_WIDTH + (hg + 1) * HG_W]
        _store_views(kh, hg, vbuf, 4 * hg, {1: k1_ref, 4: k4_ref, 16: k16_ref})
        _store_views(vh, hg, vbuf, 4 * hg + 2, {1: v1_ref, 4: v4_ref, 16: v16_ref})


def _layer0(x, p, ng, win, cw, cb, lng, lnb, wout, pg, wpg, wpp, kvg, wkv, kg, bd, cos, sa, sb):
    B, S, D = x.shape
    nt = S // TM
    row = lambda b, t: (b, t, 0)
    tab = pl.BlockSpec((TM, LANES), lambda b, t: (t, 0))
    dils = [dil for _, dil in DILATION_GROUPS for _ in range(2)]
    return pl.pallas_call(
        _layer0_kernel,
        grid=(B, nt),
        in_specs=[
            pl.BlockSpec((None, TM, D), row),
            pl.BlockSpec((None, None, TM, PLE_DIM), lambda b, t: (0, b, t, 0)),
            _const_spec(ng.shape), _const_spec(win.shape), _const_spec(cw.shape),
            _const_spec(cb.shape), _const_spec(lng.shape), _const_spec(lnb.shape),
            _const_spec(wout.shape), _const_spec(pg.shape), _const_spec(wpg.shape),
            _const_spec(wpp.shape), _const_spec(kvg.shape), _const_spec(wkv.shape),
            _const_spec(kg.shape), _const_spec(bd.shape),
            tab, tab, tab,
        ],
        out_specs=[pl.BlockSpec((None, TM, D), row)] + [_view_spec(dil) for dil in dils],
        out_shape=[jax.ShapeDtypeStruct((B, S, D), F32)] + [_view_shape(B, dil, BF16) for dil in dils],
        scratch_shapes=[
            pltpu.VMEM((N_SLAB, HALO + TM, LANES), F32),
            pltpu.VMEM((N_SLAB, TM, LANES), F32),
            pltpu.VMEM((N_SLAB, TM, LANES), F32),
            pltpu.VMEM((TM, CONV_CH), BF16),
            pltpu.VMEM((4 * N_HG, TM, LANES), F32),
        ],
        compiler_params=pltpu.CompilerParams(
            dimension_semantics=("arbitrary", "arbitrary"), vmem_limit_bytes=VMEM_LIMIT),
        name="layer0",
    )(x, p, ng, win, cw, cb, lng, lnb, wout, pg, wpg, wpp, kvg, wkv, kg, bd, cos, sa, sb)


def _qproj_kernel(h_ref, ng_ref, win_ref, qg_ref, bd_ref, cos_ref, sa_ref, sb_ref,
                  q1_ref, q2_ref, q3_ref, gate_ref, vbuf):
    u = _rmsnorm(h_ref[...], ng_ref[...]).astype(BF16)
    bd = bd_ref[...]
    cos, sa, sb = cos_ref[...], sa_ref[...], sb_ref[...]
    for g, q_ref in enumerate((q1_ref, q2_ref, q3_ref)):
        q = jnp.dot(u, win_ref[:, g * ATTN_WIDTH:(g + 1) * ATTN_WIDTH], preferred_element_type=F32)
        for hg in range(N_HG):
            qh = _head_norm_rope(q[:, hg * HG_W:(hg + 1) * HG_W], bd, qg_ref[g:g + 1, :],
                                 cos, sa, sb, HEAD_DIM ** -0.5)
            _store_views(qh, hg, vbuf, 2 * (N_HG * (g % 2) + hg), {DILATION_GROUPS[g][1]: q_ref})
    z = jnp.dot(u, win_ref[:, N_GROUPS * ATTN_WIDTH:], preferred_element_type=F32)
    gate_ref[...] = (z * _sigmoid(z)).astype(BF16)


def _qproj(h, ng, win, qg, bd, cos, sa, sb):
    B, S, D = h.shape
    row = lambda b, t: (b, t, 0)
    tab = pl.BlockSpec((TM, LANES), lambda b, t: (t, 0))
    dils = [dil for _, dil in DILATION_GROUPS]
    return pl.pallas_call(
        _qproj_kernel,
        grid=(B, S // TM),
        in_specs=[pl.BlockSpec((None, TM, D), row), _const_spec(ng.shape), _const_spec(win.shape),
                  _const_spec(qg.shape), _const_spec(bd.shape), tab, tab, tab],
        out_specs=[_view_spec(dil) for dil in dils] + [pl.BlockSpec((None, TM, ATTN_WIDTH), row)],
        out_shape=[_view_shape(B, dil, BF16) for dil in dils]
        + [jax.ShapeDtypeStruct((B, S, ATTN_WIDTH), BF16)],
        scratch_shapes=[pltpu.VMEM((4 * N_HG, TM, LANES), F32)],
        compiler_params=pltpu.CompilerParams(
            dimension_semantics=("arbitrary", "arbitrary"), vmem_limit_bytes=VMEM_LIMIT),
        name="qproj",
    )(h, ng, win, qg, bd, cos, sa, sb)


def _attn_kernel(q_ref, k_ref, v_ref, o_ref, lse_ref, *, dil, nb):
    lane = lax.broadcasted_iota(jnp.int32, (1, HG_W), 1)
    head_mask = [(lane >= h * HEAD_DIM) & (lane < (h + 1) * HEAD_DIM) for h in range(HEAD_GROUP)]
    row = lax.broadcasted_iota(jnp.int32, (SPAN, 2 * SPAN), 0)
    col = lax.broadcasted_iota(jnp.int32, (SPAN, 2 * SPAN), 1)
    band = (col >= row) & (col <= row + SPAN)
    tri = (lax.broadcasted_iota(jnp.int32, (SPAN, SPAN), 1)
           <= lax.broadcasted_iota(jnp.int32, (SPAN, SPAN), 0))

    def select_heads(vals):
        out = vals[HEAD_GROUP - 1]
        for h in range(HEAD_GROUP - 2, -1, -1):
            out = jnp.where(head_mask[h], vals[h], out)
        return out

    for r in range(dil):
        cs = slice(r * HG_W, (r + 1) * HG_W)
        for j in range(nb):
            rows = slice(j * SPAN, (j + 1) * SPAN)
            q = q_ref[rows, cs]
            if j == 0:
                kk, vv, valid = k_ref[rows, cs], v_ref[rows, cs], tri
            else:
                both = slice((j - 1) * SPAN, (j + 1) * SPAN)
                kk, vv, valid = k_ref[both, cs], v_ref[both, cs], band
            qs = jnp.concatenate(
                [jnp.where(head_mask[h], q, jnp.zeros_like(q)) for h in range(HEAD_GROUP)], axis=0)
            s = lax.dot_general(qs, kk, (((1,), (1,)), ((), ())), preferred_element_type=F32)
            ps, ms, ls = [], [], []
            for h in range(HEAD_GROUP):
                sh = jnp.where(valid, s[h * SPAN:(h + 1) * SPAN], NEG_INF)
                m = jnp.max(sh, axis=-1, keepdims=True)
                p = jnp.exp(sh - m)
                ls.append(jnp.sum(p, axis=-1, keepdims=True))
                ms.append(m)
                ps.append(p.astype(BF16))
            of = jnp.dot(jnp.concatenate(ps, axis=0), vv, preferred_element_type=F32)
            o = select_heads([of[h * SPAN:(h + 1) * SPAN] for h in range(HEAD_GROUP)])
            l = select_heads([jnp.broadcast_to(x, (SPAN, HG_W)) for x in ls])
            m = select_heads([jnp.broadcast_to(x, (SPAN, HG_W)) for x in ms])
            o_ref[rows, cs] = (o / l).astype(BF16)
            lse_ref[rows, cs] = m + jnp.log(l)


def _attention(q, k, v, dil):
    B, nhg, L, _ = q.shape
    nb = L // SPAN
    spec = pl.BlockSpec((None, None, L, dil * HG_W), lambda b, g: (b, g, 0, 0))
    return pl.pallas_call(
        functools.partial(_attn_kernel, dil=dil, nb=nb),
        grid=(B, nhg),
        in_specs=[spec, spec, spec],
        out_specs=[spec, spec],
        out_shape=[jax.ShapeDtypeStruct((B, nhg, L, dil * HG_W), BF16),
                   jax.ShapeDtypeStruct((B, nhg, L, dil * HG_W), F32)],
        compiler_params=pltpu.CompilerParams(
            dimension_semantics=("arbitrary", "arbitrary"), vmem_limit_bytes=VMEM_LIMIT),
        name=f"attn_d{dil}",
    )(q, k, v)


def _layer1_out_kernel(o1_ref, o2_ref, o3_ref, l1_ref, l2_ref, l3_ref, gate_ref, h_ref, p_ref,
                       wout_ref, pg_ref, wpg_ref, wpp_ref, out_ref, vbuf):
    cols = []
    dils = [dil for _, dil in DILATION_GROUPS]
    for hg in range(N_HG):
        ls = [_load_view(ref, hg, dil, vbuf, 4 * g + 2)
              for g, (ref, dil) in enumerate(zip((l1_ref, l2_ref, l3_ref), dils))]
        os = [_load_view(ref, hg, dil, vbuf, 4 * g)
              for g, (ref, dil) in enumerate(zip((o1_ref, o2_ref, o3_ref), dils))]
        m = jnp.maximum(jnp.maximum(ls[0], ls[1]), ls[2])
        es = [jnp.exp(l - m) for l in ls]
        num = es[0] * os[0] + es[1] * os[1] + es[2] * os[2]
        cols.append(num / (es[0] + es[1] + es[2]))
    o = jnp.concatenate(cols, axis=1)
    y = jnp.dot((o * gate_ref[...].astype(F32)).astype(BF16), wout_ref[...],
                preferred_element_type=F32)
    hmid = h_ref[...] + y
    out_ref[...] = _ple(hmid, p_ref[...], pg_ref[...], wpg_ref[...], wpp_ref[...])


def _layer1_out(os, lses, gate, h, p, wout, pg, wpg, wpp):
    B, S, D = h.shape
    row = lambda b, t: (b, t, 0)
    views = [_view_spec(dil) for _, dil in DILATION_GROUPS]
    return pl.pallas_call(
        _layer1_out_kernel,
        grid=(B, S // TM),
        in_specs=views + views + [
            pl.BlockSpec((None, TM, ATTN_WIDTH), row),
            pl.BlockSpec((None, TM, D), row),
            pl.BlockSpec((None, None, TM, PLE_DIM), lambda b, t: (1, b, t, 0)),
            _const_spec(wout.shape), _const_spec(pg.shape), _const_spec(wpg.shape),
            _const_spec(wpp.shape)],
        out_specs=pl.BlockSpec((None, TM, D), row),
        out_shape=jax.ShapeDtypeStruct((B, S, D), F32),
        scratch_shapes=[pltpu.VMEM((4 * N_GROUPS, TM, LANES), F32)],
        compiler_params=pltpu.CompilerParams(
            dimension_semantics=("arbitrary", "arbitrary"), vmem_limit_bytes=VMEM_LIMIT),
        name="layer1_out",
    )(*os, *lses, gate, h, p, wout, pg, wpg, wpp)


def _rope_tables(seq):
    half = ROPE_DIM // 2
    inv = ROPE_THETA ** (-jnp.arange(half, dtype=F32) * (2.0 / ROPE_DIM))
    ang = jnp.arange(seq).astype(F32)[:, None] * inv[None, :]
    cos, sin = jnp.cos(ang), jnp.sin(ang)
    ones = jnp.ones((seq, HEAD_DIM - ROPE_DIM), F32)
    zeros = jnp.zeros((seq, HEAD_DIM - ROPE_DIM), F32)
    zh = jnp.zeros((seq, half), F32)
    per_head = lambda parts: jnp.tile(jnp.concatenate(parts, axis=1), (1, LANES // HEAD_DIM))
    return (per_head([cos, cos, ones]), per_head([-sin, zh, zeros]), per_head([zh, sin, zeros]))


def kernel(x, p, norm_g, w_in_a, conv_w, conv_b, ln_g, ln_b, w_out_a, kv_norm_g, w_kv, k_norm_g,
           w_in_b, q_norm_g, w_out_b, ple_norm_g, w_ple_gate, w_ple_proj):
    B, S, D = x.shape
    assert (S, D) == (SEQ, D_MODEL) and S % TM == 0
    cos, sa, sb = _rope_tables(S)
    head_id = jnp.arange(HG_W) // HEAD_DIM
    bd = (head_id[:, None] == head_id[None, :]).astype(BF16)
    row = lambda a: a.reshape(1, -1)
    slabs = lambda a: a.reshape(N_SLAB, 1, LANES)
    cw = jnp.pad(conv_w[0], ((0, HALO - CONV_WIDTH), (0, 0)))
    cw = cw.reshape(HALO, N_SLAB, LANES).transpose(1, 0, 2)

    h1, *kvs = _layer0(
        x, p, row(norm_g[0]), w_in_a[0].astype(BF16), cw, slabs(conv_b[0]), slabs(ln_g[0]),
        slabs(ln_b[0]), w_out_a[0].astype(BF16), row(ple_norm_g[0]), w_ple_gate[0].astype(BF16),
        w_ple_proj[0].astype(BF16), row(kv_norm_g), w_kv.astype(BF16),
        row(jnp.tile(k_norm_g, HEAD_GROUP)), bd, cos, sa, sb)

    q1, q2, q3, gate = _qproj(h1, row(norm_g[1]), w_in_b[0].astype(BF16),
                              jnp.tile(q_norm_g[0], (1, HEAD_GROUP)), bd, cos, sa, sb)
    os, lses = [], []
    for g, q in enumerate((q1, q2, q3)):
        o, lse = _attention(q, kvs[2 * g], kvs[2 * g + 1], DILATION_GROUPS[g][1])
        os.append(o)
        lses.append(lse)
    return _layer1_out(os, lses, gate, h1, p, w_out_b[0].astype(BF16), row(ple_norm_g[1]),
                       w_ple_gate[1].astype(BF16), w_ple_proj[1].astype(BF16))
```

```python
import functools
import math

import jax
import jax.numpy as jnp
import numpy as np
from jax import lax
from jax.experimental import pallas as pl
from jax.experimental.pallas import tpu as pltpu

D_MODEL = 1024
SEQ = 2048
PLE_DIM = 256
CONV_WIDTH = 31
CONV_CH = 2048
HEAD_DIM = 64
N_HEADS = 16
ATTN_WIDTH = 1024
DILATION_GROUPS = ((128, 1), (512, 4), (2048, 16))
N_GROUPS = 3
ROPE_THETA = 500000.0
ROPE_DIM = 16
EPS = 1e-6
NEG_INF = -1e30

LANES = 128
HEAD_GROUP = 4
HG_W = HEAD_GROUP * HEAD_DIM
N_HG = N_HEADS // HEAD_GROUP
SPAN = 128
TM = 256
HALO = 32
CONV_ROWS = 64
N_SLAB = CONV_CH // LANES
IN_CHUNK = 512
VMEM_LIMIT = 56 * 1024 * 1024

ROPE_HALF = ROPE_DIM // 2
ROPE_LANES = HEAD_GROUP * ROPE_HALF
PASS_LANES = (HEAD_DIM - ROPE_DIM) // 2
LOG2E = math.log2(math.e)
LN2 = math.log(2.0)

F32 = jnp.float32
BF16 = jnp.bfloat16


def _qk_lane_source():
    src = np.zeros(HG_W, np.int32)
    for lane in range(HG_W):
        half, la = divmod(lane, LANES)
        if la < ROPE_LANES:
            h, i = divmod(la, ROPE_HALF)
            d = i + ROPE_HALF * half
        else:
            h, pd = divmod(la - ROPE_LANES, PASS_LANES)
            d = ROPE_DIM + pd + PASS_LANES * half
        src[lane] = h * HEAD_DIM + d
    return src


def _qk_head_masks(shape):
    la = lax.broadcasted_iota(jnp.int32, shape, len(shape) - 1) & (LANES - 1)
    return [((la >= ROPE_HALF * h) & (la < ROPE_HALF * (h + 1)))
            | ((la >= ROPE_LANES + PASS_LANES * h) & (la < ROPE_LANES + PASS_LANES * (h + 1)))
            for h in range(HEAD_GROUP)]


def _const_spec(shape):
    nd = len(shape)
    return pl.BlockSpec(shape, lambda *_: (0,) * nd, pipeline_mode=pl.Buffered(1))


def _rmsnorm(x, g):
    ms = jnp.sum(x * x, axis=-1, keepdims=True) * (1.0 / x.shape[-1])
    return x * lax.rsqrt(ms + EPS) * g


def _sigmoid(x):
    return 0.5 * jnp.tanh(0.5 * x) + 0.5


def _silu(x):
    h = 0.5 * x
    return h * jnp.tanh(h) + h


def _head_norm_rope(x, bd, g, cos, sin, scale):
    ss = jnp.dot((x * x).astype(BF16), bd, preferred_element_type=F32)
    xn = x * lax.rsqrt(ss * (1.0 / HEAD_DIM) + EPS) * g
    a, b = xn[:, :LANES], xn[:, LANES:]
    out = jnp.concatenate([a * cos - b * sin, b * cos + a * sin], axis=1)
    return out if scale == 1.0 else out * scale


def _view_spec(dil):
    return pl.BlockSpec((None, N_HG, TM // dil, dil * HG_W), lambda b, t: (b, 0, t, 0))


def _view_shape(batch, dil, dtype):
    return jax.ShapeDtypeStruct((batch, N_HG, SEQ // dil, dil * HG_W), dtype)


def _store_views(x, hg, scr, slab0, refs):
    halves = HG_W // LANES
    if any(dil > 1 for dil in refs):
        for s in range(halves):
            scr[slab0 + s] = x[:, s * LANES:(s + 1) * LANES]
    for dil, ref in refs.items():
        if dil == 1:
            ref[hg] = x.astype(BF16)
            continue
        for r in range(dil):
            for s in range(halves):
                piece = scr[slab0 + s, pl.ds(r, TM // dil, stride=dil), :]
                ref[hg, :, r * HG_W + s * LANES:r * HG_W + (s + 1) * LANES] = piece.astype(BF16)


def _load_view(ref, hg, dil, scr, slab0):
    if dil == 1:
        return ref[hg].astype(F32)
    halves = HG_W // LANES
    for r in range(dil):
        for s in range(halves):
            piece = ref[hg, :, r * HG_W + s * LANES:r * HG_W + (s + 1) * LANES]
            scr[slab0 + s, pl.ds(r, TM // dil, stride=dil), :] = piece.astype(F32)
    return jnp.concatenate([scr[slab0 + s] for s in range(halves)], axis=1)


def _ple(hmid, p, pg, wpg, wpp):
    gate = _sigmoid(jnp.dot(_rmsnorm(hmid, pg).astype(BF16), wpg, preferred_element_type=F32))
    pe = jnp.dot(p.astype(BF16), wpp, preferred_element_type=F32)
    return hmid + gate * pe


def _layer0_kernel(x_ref, p_ref, ng_ref, win_ref, cw_ref, cb_ref, lng_ref, lnb_ref, wout_ref,
                   pg_ref, wpg_ref, wpp_ref, kvg_ref, wkv_ref, kg_ref, bd_ref, cos_ref, sin_ref,
                   h_ref, k1_ref, v1_ref, k4_ref, v4_ref, k16_ref, v16_ref,
                   gbuf, ybuf, zbuf, mbuf, vbuf):
    t = pl.program_id(1)

    @pl.when(t == 0)
    def _():
        gbuf[:, 0:HALO, :] = jnp.zeros((N_SLAB, HALO, LANES), F32)

    x = x_ref[...]
    u = _rmsnorm(x, ng_ref[...]).astype(BF16)

    for c in range(CONV_CH // IN_CHUNK):
        lo = c * IN_CHUNK
        a = jnp.dot(u, win_ref[:, lo:lo + IN_CHUNK], preferred_element_type=F32)
        b = jnp.dot(u, win_ref[:, CONV_CH + lo:CONV_CH + lo + IN_CHUNK], preferred_element_type=F32)
        z = jnp.dot(u, win_ref[:, 2 * CONV_CH + lo:2 * CONV_CH + lo + IN_CHUNK],
                    preferred_element_type=F32)
        glu = a * _sigmoid(b)
        sz = _silu(z)
        for s in range(IN_CHUNK // LANES):
            slab = c * (IN_CHUNK // LANES) + s
            gbuf[slab, HALO:HALO + TM, :] = glu[:, s * LANES:(s + 1) * LANES]
            zbuf[slab] = sz[:, s * LANES:(s + 1) * LANES]

    tap0 = HALO - (CONV_WIDTH - 1)

    def conv_slab(s, carry):
        w = cw_ref[s]
        bias = cb_ref[s]
        for rb in range(TM // CONV_ROWS):
            acc = jnp.zeros((CONV_ROWS, LANES), F32)
            for k in range(CONV_WIDTH):
                acc = acc + gbuf[s, pl.ds(rb * CONV_ROWS + tap0 + k, CONV_ROWS), :] * w[k:k + 1, :]
            ybuf[s, pl.ds(rb * CONV_ROWS, CONV_ROWS), :] = acc + bias
        return carry

    lax.fori_loop(0, N_SLAB, conv_slab, 0)

    gbuf[:, 0:HALO, :] = gbuf[:, TM:TM + HALO, :]

    tot = ybuf[0]
    for s in range(1, N_SLAB):
        tot = tot + ybuf[s]
    mu = jnp.sum(tot, axis=-1, keepdims=True) * (1.0 / CONV_CH)
    d0 = ybuf[0] - mu
    vs = d0 * d0
    for s in range(1, N_SLAB):
        ds = ybuf[s] - mu
        vs = vs + ds * ds
    rstd = lax.rsqrt(jnp.sum(vs, axis=-1, keepdims=True) * (1.0 / CONV_CH) + EPS)
    for s in range(N_SLAB):
        yn = (ybuf[s] - mu) * rstd * lng_ref[s] + lnb_ref[s]
        mbuf[:, s * LANES:(s + 1) * LANES] = (_silu(yn) * zbuf[s]).astype(BF16)
    hmid = x + jnp.dot(mbuf[...], wout_ref[...], preferred_element_type=F32)

    h = _ple(hmid, p_ref[...], pg_ref[...], wpg_ref[...], wpp_ref[...])
    h_ref[...] = h

    kv = jnp.dot(_rmsnorm(h, kvg_ref[...]).astype(BF16), wkv_ref[...], preferred_element_type=F32)
    bd = bd_ref[...]
    cos, sin = cos_ref[...], sin_ref[...]
    for hg in range(N_HG):
        kh = kv[:, hg * HG_W:(hg + 1) * HG_W]
        kh = _head_norm_rope(kh, bd, kg_ref[...], cos, sin, 1.0)
        vh = kv[:, ATTN_WIDTH + hg * HG_W:ATTN_WIDTH + (hg + 1) * HG_W]
        _store_views(kh, hg, vbuf, 4 * hg, {1: k1_ref, 4: k4_ref, 16: k16_ref})
        _store_views(vh, hg, vbuf, 4 * hg + 2, {1: v1_ref, 4: v4_ref, 16: v16_ref})


def _layer0(x, p, ng, win, cw, cb, lng, lnb, wout, pg, wpg, wpp, kvg, wkv, kg, bd, cos, sin):
    B, S, D = x.shape
    nt = S // TM
    row = lambda b, t: (b, t, 0)
    tab = pl.BlockSpec((TM, LANES), lambda b, t: (t, 0))
    dils = [dil for _, dil in DILATION_GROUPS for _ in range(2)]
    return pl.pallas_call(
        _layer0_kernel,
        grid=(B, nt),
        in_specs=[
            pl.BlockSpec((None, TM, D), row),
            pl.BlockSpec((None, None, TM, PLE_DIM), lambda b, t: (0, b, t, 0)),
            _const_spec(ng.shape), _const_spec(win.shape), _const_spec(cw.shape),
            _const_spec(cb.shape), _const_spec(lng.shape), _const_spec(lnb.shape),
            _const_spec(wout.shape), _const_spec(pg.shape), _const_spec(wpg.shape),
            _const_spec(wpp.shape), _const_spec(kvg.shape), _const_spec(wkv.shape),
            _const_spec(kg.shape), _const_spec(bd.shape),
            tab, tab,
        ],
        out_specs=[pl.BlockSpec((None, TM, D), row)] + [_view_spec(dil) for dil in dils],
        out_shape=[jax.ShapeDtypeStruct((B, S, D), F32)] + [_view_shape(B, dil, BF16) for dil in dils],
        scratch_shapes=[
            pltpu.VMEM((N_SLAB, HALO + TM, LANES), F32),
            pltpu.VMEM((N_SLAB, TM, LANES), F32),
            pltpu.VMEM((N_SLAB, TM, LANES), F32),
            pltpu.VMEM((TM, CONV_CH), BF16),
            pltpu.VMEM((4 * N_HG, TM, LANES), F32),
        ],
        compiler_params=pltpu.CompilerParams(
            dimension_semantics=("arbitrary", "arbitrary"), vmem_limit_bytes=VMEM_LIMIT),
        name="layer0",
    )(x, p, ng, win, cw, cb, lng, lnb, wout, pg, wpg, wpp, kvg, wkv, kg, bd, cos, sin)


def _qproj_kernel(h_ref, ng_ref, win_ref, qg_ref, bd_ref, cos_ref, sin_ref,
                  q1_ref, q2_ref, q3_ref, gate_ref, vbuf):
    u = _rmsnorm(h_ref[...], ng_ref[...]).astype(BF16)
    bd = bd_ref[...]
    cos, sin = cos_ref[...], sin_ref[...]
    for g, q_ref in enumerate((q1_ref, q2_ref, q3_ref)):
        q = jnp.dot(u, win_ref[:, g * ATTN_WIDTH:(g + 1) * ATTN_WIDTH], preferred_element_type=F32)
        for hg in range(N_HG):
            qh = _head_norm_rope(q[:, hg * HG_W:(hg + 1) * HG_W], bd, qg_ref[g:g + 1, :],
                                 cos, sin, HEAD_DIM ** -0.5 * LOG2E)
            _store_views(qh, hg, vbuf, 2 * (N_HG * (g % 2) + hg), {DILATION_GROUPS[g][1]: q_ref})
    z = jnp.dot(u, win_ref[:, N_GROUPS * ATTN_WIDTH:], preferred_element_type=F32)
    gate_ref[...] = _silu(z).astype(BF16)


def _qproj(h, ng, win, qg, bd, cos, sin):
    B, S, D = h.shape
    row = lambda b, t: (b, t, 0)
    tab = pl.BlockSpec((TM, LANES), lambda b, t: (t, 0))
    dils = [dil for _, dil in DILATION_GROUPS]
    return pl.pallas_call(
        _qproj_kernel,
        grid=(B, S // TM),
        in_specs=[pl.BlockSpec((None, TM, D), row), _const_spec(ng.shape), _const_spec(win.shape),
                  _const_spec(qg.shape), _const_spec(bd.shape), tab, tab],
        out_specs=[_view_spec(dil) for dil in dils] + [pl.BlockSpec((None, TM, ATTN_WIDTH), row)],
        out_shape=[_view_shape(B, dil, BF16) for dil in dils]
        + [jax.ShapeDtypeStruct((B, S, ATTN_WIDTH), BF16)],
        scratch_shapes=[pltpu.VMEM((4 * N_HG, TM, LANES), F32)],
        compiler_params=pltpu.CompilerParams(
            dimension_semantics=("arbitrary", "arbitrary"), vmem_limit_bytes=VMEM_LIMIT),
        name="qproj",
    )(h, ng, win, qg, bd, cos, sin)


def _attn_kernel(q_ref, k_ref, v_ref, o_ref, lse_ref, *, dil, nb):
    qk_mask = _qk_head_masks((1, HG_W))
    lane = lax.broadcasted_iota(jnp.int32, (1, HG_W), 1)
    v_mask = [(lane >= h * HEAD_DIM) & (lane < (h + 1) * HEAD_DIM) for h in range(HEAD_GROUP)]
    row = lax.broadcasted_iota(jnp.int32, (SPAN, 2 * SPAN), 0)
    col = lax.broadcasted_iota(jnp.int32, (SPAN, 2 * SPAN), 1)
    band = (col >= row) & (col <= row + SPAN)
    tri = (lax.broadcasted_iota(jnp.int32, (SPAN, SPAN), 1)
           <= lax.broadcasted_iota(jnp.int32, (SPAN, SPAN), 0))

    def select_heads(vals):
        out = vals[HEAD_GROUP - 1]
        for h in range(HEAD_GROUP - 2, -1, -1):
            out = jnp.where(v_mask[h], vals[h], out)
        return out

    for r in range(dil):
        cs = slice(r * HG_W, (r + 1) * HG_W)
        for j in range(nb):
            rows = slice(j * SPAN, (j + 1) * SPAN)
            q = q_ref[rows, cs]
            if j == 0:
                kk, vv, valid = k_ref[rows, cs], v_ref[rows, cs], tri
            else:
                both = slice((j - 1) * SPAN, (j + 1) * SPAN)
                kk, vv, valid = k_ref[both, cs], v_ref[both, cs], band
            qs = jnp.concatenate(
                [jnp.where(qk_mask[h], q, jnp.zeros_like(q)) for h in range(HEAD_GROUP)], axis=0)
            s = lax.dot_general(qs, kk, (((1,), (1,)), ((), ())), preferred_element_type=F32)
            ps, ms, ls = [], [], []
            for h in range(HEAD_GROUP):
                sh = jnp.where(valid, s[h * SPAN:(h + 1) * SPAN], NEG_INF)
                m = jnp.max(sh, axis=-1, keepdims=True)
                p = jnp.exp2(sh - m)
                ls.append(jnp.sum(p, axis=-1, keepdims=True))
                ms.append(m)
                ps.append(p.astype(BF16))
            of = jnp.dot(jnp.concatenate(ps, axis=0), vv, preferred_element_type=F32)
            o = select_heads([of[h * SPAN:(h + 1) * SPAN] for h in range(HEAD_GROUP)])
            l = select_heads([jnp.broadcast_to(x, (SPAN, HG_W)) for x in ls])
            m = select_heads([jnp.broadcast_to(x, (SPAN, HG_W)) for x in ms])
            o_ref[rows, cs] = (o / l).astype(BF16)
            lse_ref[rows, cs] = m * LN2 + jnp.log(l)


def _attention(q, k, v, dil):
    B, nhg, L, _ = q.shape
    nb = L // SPAN
    spec = pl.BlockSpec((None, None, L, dil * HG_W), lambda b, g: (b, g, 0, 0))
    return pl.pallas_call(
        functools.partial(_attn_kernel, dil=dil, nb=nb),
        grid=(B, nhg),
        in_specs=[spec, spec, spec],
        out_specs=[spec, spec],
        out_shape=[jax.ShapeDtypeStruct((B, nhg, L, dil * HG_W), BF16),
                   jax.ShapeDtypeStruct((B, nhg, L, dil * HG_W), F32)],
        compiler_params=pltpu.CompilerParams(
            dimension_semantics=("arbitrary", "arbitrary"), vmem_limit_bytes=VMEM_LIMIT),
        name=f"attn_d{dil}",
    )(q, k, v)


def _layer1_out_kernel(o1_ref, o2_ref, o3_ref, l1_ref, l2_ref, l3_ref, gate_ref, h_ref, p_ref,
                       wout_ref, pg_ref, wpg_ref, wpp_ref, out_ref, vbuf):
    cols = []
    dils = [dil for _, dil in DILATION_GROUPS]
    for hg in range(N_HG):
        ls = [_load_view(ref, hg, dil, vbuf, 4 * g + 2)
              for g, (ref, dil) in enumerate(zip((l1_ref, l2_ref, l3_ref), dils))]
        os = [_load_view(ref, hg, dil, vbuf, 4 * g)
              for g, (ref, dil) in enumerate(zip((o1_ref, o2_ref, o3_ref), dils))]
        m = jnp.maximum(jnp.maximum(ls[0], ls[1]), ls[2])
        es = [jnp.exp(l - m) for l in ls]
        num = es[0] * os[0] + es[1] * os[1] + es[2] * os[2]
        cols.append(num / (es[0] + es[1] + es[2]))
    o = jnp.concatenate(cols, axis=1)
    y = jnp.dot((o * gate_ref[...].astype(F32)).astype(BF16), wout_ref[...],
                preferred_element_type=F32)
    hmid = h_ref[...] + y
    out_ref[...] = _ple(hmid, p_ref[...], pg_ref[...], wpg_ref[...], wpp_ref[...])


def _layer1_out(os, lses, gate, h, p, wout, pg, wpg, wpp):
    B, S, D = h.shape
    row = lambda b, t: (b, t, 0)
    views = [_view_spec(dil) for _, dil in DILATION_GROUPS]
    return pl.pallas_call(
        _layer1_out_kernel,
        grid=(B, S // TM),
        in_specs=views + views + [
            pl.BlockSpec((None, TM, ATTN_WIDTH), row),
            pl.BlockSpec((None, TM, D), row),
            pl.BlockSpec((None, None, TM, PLE_DIM), lambda b, t: (1, b, t, 0)),
            _const_spec(wout.shape), _const_spec(pg.shape), _const_spec(wpg.shape),
            _const_spec(wpp.shape)],
        out_specs=pl.BlockSpec((None, TM, D), row),
        out_shape=jax.ShapeDtypeStruct((B, S, D), F32),
        scratch_shapes=[pltpu.VMEM((4 * N_GROUPS, TM, LANES), F32)],
        compiler_params=pltpu.CompilerParams(
            dimension_semantics=("arbitrary", "arbitrary"), vmem_limit_bytes=VMEM_LIMIT),
        name="layer1_out",
    )(*os, *lses, gate, h, p, wout, pg, wpg, wpp)


def _rope_tables(seq):
    inv = ROPE_THETA ** (-jnp.arange(ROPE_HALF, dtype=F32) * (2.0 / ROPE_DIM))
    ang = jnp.arange(seq).astype(F32)[:, None] * inv[None, :]
    rest = LANES - ROPE_LANES
    cos = jnp.concatenate([jnp.tile(jnp.cos(ang), (1, HEAD_GROUP)), jnp.ones((seq, rest), F32)], axis=1)
    sin = jnp.concatenate([jnp.tile(jnp.sin(ang), (1, HEAD_GROUP)), jnp.zeros((seq, rest), F32)], axis=1)
    return cos, sin


def kernel(x, p, norm_g, w_in_a, conv_w, conv_b, ln_g, ln_b, w_out_a, kv_norm_g, w_kv, k_norm_g,
           w_in_b, q_norm_g, w_out_b, ple_norm_g, w_ple_gate, w_ple_proj):
    B, S, D = x.shape
    assert (S, D) == (SEQ, D_MODEL) and S % TM == 0
    cos, sin = _rope_tables(S)
    src = _qk_lane_source()
    head_of_lane = src // HEAD_DIM
    bd = jnp.asarray(head_of_lane[:, None] == head_of_lane[None, :], BF16)
    qk_cols = np.concatenate([hg * HG_W + src for hg in range(N_HG)])
    row = lambda a: a.reshape(1, -1)
    slabs = lambda a: a.reshape(N_SLAB, 1, LANES)
    cw = jnp.pad(conv_w[0], ((0, HALO - CONV_WIDTH), (0, 0)))
    cw = cw.reshape(HALO, N_SLAB, LANES).transpose(1, 0, 2)
    wkv = jnp.concatenate([w_kv[:, :ATTN_WIDTH][:, qk_cols], w_kv[:, ATTN_WIDTH:]], axis=1)

    h1, *kvs = _layer0(
        x, p, row(norm_g[0]), w_in_a[0].astype(BF16), cw, slabs(conv_b[0]), slabs(ln_g[0]),
        slabs(ln_b[0]), w_out_a[0].astype(BF16), row(ple_norm_g[0]), w_ple_gate[0].astype(BF16),
        w_ple_proj[0].astype(BF16), row(kv_norm_g), wkv.astype(BF16),
        row(k_norm_g[src % HEAD_DIM]), bd, cos, sin)

    win_b = jnp.concatenate(
        [w_in_b[0][:, g * ATTN_WIDTH:(g + 1) * ATTN_WIDTH][:, qk_cols] for g in range(N_GROUPS)]
        + [w_in_b[0][:, N_GROUPS * ATTN_WIDTH:]], axis=1)
    q1, q2, q3, gate = _qproj(h1, row(norm_g[1]), win_b.astype(BF16),
                              q_norm_g[0][:, src % HEAD_DIM], bd, cos, sin)
    os, lses = [], []
    for g, q in enumerate((q1, q2, q3)):
        o, lse = _attention(q, kvs[2 * g], kvs[2 * g + 1], DILATION_GROUPS[g][1])
        os.append(o)
        lses.append(lse)
    return _layer1_out(os, lses, gate, h1, p, w_out_b[0].astype(BF16), row(ple_norm_g[1]),
                       w_ple_gate[1].astype(BF16), w_ple_proj[1].astype(BF16))
```

```python
import functools
import math

import jax
import jax.numpy as jnp
import numpy as np
from jax import lax
from jax.experimental import pallas as pl
from jax.experimental.pallas import tpu as pltpu

D_MODEL = 1024
SEQ = 2048
PLE_DIM = 256
CONV_WIDTH = 31
CONV_CH = 2048
HEAD_DIM = 64
N_HEADS = 16
ATTN_WIDTH = 1024
DILATION_GROUPS = ((128, 1), (512, 4), (2048, 16))
N_GROUPS = 3
ROPE_THETA = 500000.0
ROPE_DIM = 16
EPS = 1e-6
NEG_INF = -1e30

LANES = 128
HEAD_GROUP = 4
HG_W = HEAD_GROUP * HEAD_DIM
N_HG = N_HEADS // HEAD_GROUP
SPAN = 128
TM = 256
HALO = 32
CONV_ROWS = 64
N_SLAB = CONV_CH // LANES
IN_CHUNK = 512
VMEM_LIMIT = 56 * 1024 * 1024

ROPE_HALF = ROPE_DIM // 2
ROPE_LANES = HEAD_GROUP * ROPE_HALF
PASS_LANES = (HEAD_DIM - ROPE_DIM) // 2
LOG2E = math.log2(math.e)
LN2 = math.log(2.0)

F32 = jnp.float32
BF16 = jnp.bfloat16


def _qk_lane_source():
    src = np.zeros(HG_W, np.int32)
    for lane in range(HG_W):
        half, la = divmod(lane, LANES)
        if la < ROPE_LANES:
            h, i = divmod(la, ROPE_HALF)
            d = i + ROPE_HALF * half
        else:
            h, pd = divmod(la - ROPE_LANES, PASS_LANES)
            d = ROPE_DIM + pd + PASS_LANES * half
        src[lane] = h * HEAD_DIM + d
    return src


def _qk_head_masks(shape):
    la = lax.broadcasted_iota(jnp.int32, shape, len(shape) - 1) & (LANES - 1)
    return [((la >= ROPE_HALF * h) & (la < ROPE_HALF * (h + 1)))
            | ((la >= ROPE_LANES + PASS_LANES * h) & (la < ROPE_LANES + PASS_LANES * (h + 1)))
            for h in range(HEAD_GROUP)]


def _const_spec(shape):
    nd = len(shape)
    return pl.BlockSpec(shape, lambda *_: (0,) * nd, pipeline_mode=pl.Buffered(1))


def _rmsnorm(x, g):
    ms = jnp.sum(x * x, axis=-1, keepdims=True) * (1.0 / x.shape[-1])
    return x * lax.rsqrt(ms + EPS) * g


def _sigmoid(x):
    return 0.5 * jnp.tanh(0.5 * x) + 0.5


def _silu(x):
    h = 0.5 * x
    return h * jnp.tanh(h) + h


def _head_norm_rope(x, bd, g, cos, sin, scale):
    ss = jnp.dot((x * x).astype(BF16), bd, preferred_element_type=F32)
    xn = x * lax.rsqrt(ss * (1.0 / HEAD_DIM) + EPS) * g
    a, b = xn[:, :LANES], xn[:, LANES:]
    out = jnp.concatenate([a * cos - b * sin, b * cos + a * sin], axis=1)
    return out if scale == 1.0 else out * scale


def _view_spec(dil):
    return pl.BlockSpec((None, N_HG, TM // dil, dil * HG_W), lambda b, t: (b, 0, t, 0))


def _view_shape(batch, dil, dtype):
    return jax.ShapeDtypeStruct((batch, N_HG, SEQ // dil, dil * HG_W), dtype)


def _store_views(x, hg, scr, slab0, refs):
    halves = HG_W // LANES
    if any(dil > 1 for dil in refs):
        for s in range(halves):
            scr[slab0 + s] = x[:, s * LANES:(s + 1) * LANES]
    for dil, ref in refs.items():
        if dil == 1:
            ref[hg] = x.astype(BF16)
            continue
        for r in range(dil):
            for s in range(halves):
                piece = scr[slab0 + s, pl.ds(r, TM // dil, stride=dil), :]
                ref[hg, :, r * HG_W + s * LANES:r * HG_W + (s + 1) * LANES] = piece.astype(BF16)


def _load_view(ref, hg, dil, scr, slab0):
    if dil == 1:
        return ref[hg].astype(F32)
    halves = HG_W // LANES
    for r in range(dil):
        for s in range(halves):
            piece = ref[hg, :, r * HG_W + s * LANES:r * HG_W + (s + 1) * LANES]
            scr[slab0 + s, pl.ds(r, TM // dil, stride=dil), :] = piece.astype(F32)
    return jnp.concatenate([scr[slab0 + s] for s in range(halves)], axis=1)


def _ple(hmid, p, pg, wpg, wpp):
    gate = _sigmoid(jnp.dot(_rmsnorm(hmid, pg).astype(BF16), wpg, preferred_element_type=F32))
    pe = jnp.dot(p.astype(BF16), wpp, preferred_element_type=F32)
    return hmid + gate * pe


def _layer0_kernel(x_ref, xn_ref, p_ref, ng_ref, wab_ref, wz_ref, cw_ref, cb_ref, lng_ref, lnb_ref,
                   wout_ref, pg_ref, wpg_ref, wpp_ref, kvg_ref, wkv_ref, kg_ref, bd_ref,
                   cos_ref, sin_ref,
                   h_ref, k1_ref, v1_ref, k4_ref, v4_ref, k16_ref, v16_ref,
                   gbuf, ybuf, zbuf, mbuf, vbuf, ubuf, rawbuf, *, n_seq_tiles):
    i = pl.program_id(0)
    cur = lax.rem(i, 2)
    nxt = 1 - cur

    @pl.when(lax.rem(i, n_seq_tiles) == 0)
    def _():
        gbuf[:, 0:HALO, :] = jnp.zeros((N_SLAB, HALO, LANES), F32)

    @pl.when(i == 0)
    def _():
        ubuf[0] = _rmsnorm(x_ref[...], ng_ref[...]).astype(BF16)
        for s in range(N_SLAB):
            rawbuf[s] = jnp.dot(ubuf[0], wab_ref[s], preferred_element_type=F32)

    x = x_ref[...]
    u = ubuf[cur]
    ubuf[nxt] = _rmsnorm(xn_ref[...], ng_ref[...]).astype(BF16)

    n_chunk = N_SLAB // 2
    for c in range(n_chunk):
        glu = rawbuf[c] * _sigmoid(rawbuf[n_chunk + c])
        for s in range(2):
            gbuf[2 * c + s, HALO:HALO + TM, :] = glu[:, s * LANES:(s + 1) * LANES]

    for c in range(CONV_CH // IN_CHUNK):
        z = jnp.dot(u, wz_ref[:, c * IN_CHUNK:(c + 1) * IN_CHUNK], preferred_element_type=F32)
        sz = _silu(z)
        for s in range(IN_CHUNK // LANES):
            zbuf[c * (IN_CHUNK // LANES) + s] = sz[:, s * LANES:(s + 1) * LANES]

    tap0 = HALO - (CONV_WIDTH - 1)

    def conv_slab(s, carry):
        w = cw_ref[s]
        bias = cb_ref[s]
        for rb in range(TM // CONV_ROWS):
            acc = jnp.zeros((CONV_ROWS, LANES), F32)
            for k in range(CONV_WIDTH):
                acc = acc + gbuf[s, pl.ds(rb * CONV_ROWS + tap0 + k, CONV_ROWS), :] * w[k:k + 1, :]
            ybuf[s, pl.ds(rb * CONV_ROWS, CONV_ROWS), :] = acc + bias
        rawbuf[s] = jnp.dot(ubuf[nxt], wab_ref[s], preferred_element_type=F32)
        return carry

    lax.fori_loop(0, N_SLAB, conv_slab, 0)

    gbuf[:, 0:HALO, :] = gbuf[:, TM:TM + HALO, :]

    tot = ybuf[0]
    for s in range(1, N_SLAB):
        tot = tot + ybuf[s]
    mu = jnp.sum(tot, axis=-1, keepdims=True) * (1.0 / CONV_CH)
    d0 = ybuf[0] - mu
    vs = d0 * d0
    for s in range(1, N_SLAB):
        ds = ybuf[s] - mu
        vs = vs + ds * ds
    rstd = lax.rsqrt(jnp.sum(vs, axis=-1, keepdims=True) * (1.0 / CONV_CH) + EPS)
    for s in range(N_SLAB):
        yn = (ybuf[s] - mu) * rstd * lng_ref[s] + lnb_ref[s]
        mbuf[:, s * LANES:(s + 1) * LANES] = (_silu(yn) * zbuf[s]).astype(BF16)
    hmid = x + jnp.dot(mbuf[...], wout_ref[...], preferred_element_type=F32)

    h = _ple(hmid, p_ref[...], pg_ref[...], wpg_ref[...], wpp_ref[...])
    h_ref[...] = h

    kv = jnp.dot(_rmsnorm(h, kvg_ref[...]).astype(BF16), wkv_ref[...], preferred_element_type=F32)
    bd = bd_ref[...]
    cos, sin = cos_ref[...], sin_ref[...]
    for hg in range(N_HG):
        kh = kv[:, hg * HG_W:(hg + 1) * HG_W]
        kh = _head_norm_rope(kh, bd, kg_ref[...], cos, sin, 1.0)
        vh = kv[:, ATTN_WIDTH + hg * HG_W:ATTN_WIDTH + (hg + 1) * HG_W]
        _store_views(kh, hg, vbuf, 4 * hg, {1: k1_ref, 4: k4_ref, 16: k16_ref})
        _store_views(vh, hg, vbuf, 4 * hg + 2, {1: v1_ref, 4: v4_ref, 16: v16_ref})


def _layer0(x, p, ng, wab, wz, cw, cb, lng, lnb, wout, pg, wpg, wpp, kvg, wkv, kg, bd, cos, sin):
    B, S, D = x.shape
    nt = S // TM
    n_tiles = B * nt
    row = lambda i: (i // nt, i % nt, 0)
    nxt = lambda i: jnp.minimum(i + 1, n_tiles - 1)
    tab = pl.BlockSpec((TM, LANES), lambda i: (i % nt, 0))
    dils = [dil for _, dil in DILATION_GROUPS for _ in range(2)]
    view = lambda dil: pl.BlockSpec((None, N_HG, TM // dil, dil * HG_W),
                                    lambda i: (i // nt, 0, i % nt, 0))
    return pl.pallas_call(
        functools.partial(_layer0_kernel, n_seq_tiles=nt),
        grid=(n_tiles,),
        in_specs=[
            pl.BlockSpec((None, TM, D), row),
            pl.BlockSpec((None, TM, D), lambda i: row(nxt(i))),
            pl.BlockSpec((None, None, TM, PLE_DIM), lambda i: (0, i // nt, i % nt, 0)),
            _const_spec(ng.shape), _const_spec(wab.shape), _const_spec(wz.shape),
            _const_spec(cw.shape), _const_spec(cb.shape), _const_spec(lng.shape),
            _const_spec(lnb.shape), _const_spec(wout.shape), _const_spec(pg.shape),
            _const_spec(wpg.shape), _const_spec(wpp.shape), _const_spec(kvg.shape),
            _const_spec(wkv.shape), _const_spec(kg.shape), _const_spec(bd.shape),
            tab, tab,
        ],
        out_specs=[pl.BlockSpec((None, TM, D), row)] + [view(dil) for dil in dils],
        out_shape=[jax.ShapeDtypeStruct((B, S, D), F32)] + [_view_shape(B, dil, BF16) for dil in dils],
        scratch_shapes=[
            pltpu.VMEM((N_SLAB, HALO + TM, LANES), F32),
            pltpu.VMEM((N_SLAB, TM, LANES), F32),
            pltpu.VMEM((N_SLAB, TM, LANES), F32),
            pltpu.VMEM((TM, CONV_CH), BF16),
            pltpu.VMEM((4 * N_HG, TM, LANES), F32),
            pltpu.VMEM((2, TM, D), BF16),
            pltpu.VMEM((N_SLAB, TM, 2 * LANES), F32),
        ],
        compiler_params=pltpu.CompilerParams(
            dimension_semantics=("arbitrary",), vmem_limit_bytes=VMEM_LIMIT),
        name="layer0",
    )(x, x, p, ng, wab, wz, cw, cb, lng, lnb, wout, pg, wpg, wpp, kvg, wkv, kg, bd, cos, sin)


def _qproj_kernel(h_ref, ng_ref, win_ref, qg_ref, bd_ref, cos_ref, sin_ref,
                  q1_ref, q2_ref, q3_ref, gate_ref, vbuf):
    u = _rmsnorm(h_ref[...], ng_ref[...]).astype(BF16)
    bd = bd_ref[...]
    cos, sin = cos_ref[...], sin_ref[...]
    for g, q_ref in enumerate((q1_ref, q2_ref, q3_ref)):
        q = jnp.dot(u, win_ref[:, g * ATTN_WIDTH:(g + 1) * ATTN_WIDTH], preferred_element_type=F32)
        for hg in range(N_HG):
            qh = _head_norm_rope(q[:, hg * HG_W:(hg + 1) * HG_W], bd, qg_ref[g:g + 1, :],
                                 cos, sin, HEAD_DIM ** -0.5 * LOG2E)
            _store_views(qh, hg, vbuf, 2 * (N_HG * (g % 2) + hg), {DILATION_GROUPS[g][1]: q_ref})
    z = jnp.dot(u, win_ref[:, N_GROUPS * ATTN_WIDTH:], preferred_element_type=F32)
    gate_ref[...] = _silu(z).astype(BF16)


def _qproj(h, ng, win, qg, bd, cos, sin):
    B, S, D = h.shape
    row = lambda b, t: (b, t, 0)
    tab = pl.BlockSpec((TM, LANES), lambda b, t: (t, 0))
    dils = [dil for _, dil in DILATION_GROUPS]
    return pl.pallas_call(
        _qproj_kernel,
        grid=(B, S // TM),
        in_specs=[pl.BlockSpec((None, TM, D), row), _const_spec(ng.shape), _const_spec(win.shape),
                  _const_spec(qg.shape), _const_spec(bd.shape), tab, tab],
        out_specs=[_view_spec(dil) for dil in dils] + [pl.BlockSpec((None, TM, ATTN_WIDTH), row)],
        out_shape=[_view_shape(B, dil, BF16) for dil in dils]
        + [jax.ShapeDtypeStruct((B, S, ATTN_WIDTH), BF16)],
        scratch_shapes=[pltpu.VMEM((4 * N_HG, TM, LANES), F32)],
        compiler_params=pltpu.CompilerParams(
            dimension_semantics=("arbitrary", "arbitrary"), vmem_limit_bytes=VMEM_LIMIT),
        name="qproj",
    )(h, ng, win, qg, bd, cos, sin)


def _attn_kernel(q_ref, k_ref, v_ref, o_ref, lse_ref, *, dil, nb):
    qk_mask = _qk_head_masks((1, HG_W))
    lane = lax.broadcasted_iota(jnp.int32, (1, HG_W), 1)
    v_mask = [(lane >= h * HEAD_DIM) & (lane < (h + 1) * HEAD_DIM) for h in range(HEAD_GROUP)]
    row = lax.broadcasted_iota(jnp.int32, (SPAN, 2 * SPAN), 0)
    col = lax.broadcasted_iota(jnp.int32, (SPAN, 2 * SPAN), 1)
    band = (col >= row) & (col <= row + SPAN)
    tri = (lax.broadcasted_iota(jnp.int32, (SPAN, SPAN), 1)
           <= lax.broadcasted_iota(jnp.int32, (SPAN, SPAN), 0))

    def select_heads(vals):
        out = vals[HEAD_GROUP - 1]
        for h in range(HEAD_GROUP - 2, -1, -1):
            out = jnp.where(v_mask[h], vals[h], out)
        return out

    for r in range(dil):
        cs = slice(r * HG_W, (r + 1) * HG_W)
        for j in range(nb):
            rows = slice(j * SPAN, (j + 1) * SPAN)
            q = q_ref[rows, cs]
            if j == 0:
                kk, vv, valid = k_ref[rows, cs], v_ref[rows, cs], tri
            else:
                both = slice((j - 1) * SPAN, (j + 1) * SPAN)
                kk, vv, valid = k_ref[both, cs], v_ref[both, cs], band
            qs = jnp.concatenate(
                [jnp.where(qk_mask[h], q, jnp.zeros_like(q)) for h in range(HEAD_GROUP)], axis=0)
            s = lax.dot_general(qs, kk, (((1,), (1,)), ((), ())), preferred_element_type=F32)
            ps, ms, ls = [], [], []
            for h in range(HEAD_GROUP):
                sh = jnp.where(valid, s[h * SPAN:(h + 1) * SPAN], NEG_INF)
                m = jnp.max(sh, axis=-1, keepdims=True)
                p = jnp.exp2(sh - m)
                ls.append(jnp.sum(p, axis=-1, keepdims=True))
                ms.append(m)
                ps.append(p.astype(BF16))
            of = jnp.dot(jnp.concatenate(ps, axis=0), vv, preferred_element_type=F32)
            o = select_heads([of[h * SPAN:(h + 1) * SPAN] for h in range(HEAD_GROUP)])
            l = select_heads([jnp.broadcast_to(x, (SPAN, HG_W)) for x in ls])
            m = select_heads([jnp.broadcast_to(x, (SPAN, HG_W)) for x in ms])
            o_ref[rows, cs] = (o / l).astype(BF16)
            lse_ref[rows, cs] = m * LN2 + jnp.log(l)


def _attention(q, k, v, dil):
    B, nhg, L, _ = q.shape
    nb = L // SPAN
    spec = pl.BlockSpec((None, None, L, dil * HG_W), lambda b, g: (b, g, 0, 0))
    return pl.pallas_call(
        functools.partial(_attn_kernel, dil=dil, nb=nb),
        grid=(B, nhg),
        in_specs=[spec, spec, spec],
        out_specs=[spec, spec],
        out_shape=[jax.ShapeDtypeStruct((B, nhg, L, dil * HG_W), BF16),
                   jax.ShapeDtypeStruct((B, nhg, L, dil * HG_W), F32)],
        compiler_params=pltpu.CompilerParams(
            dimension_semantics=("arbitrary", "arbitrary"), vmem_limit_bytes=VMEM_LIMIT),
        name=f"attn_d{dil}",
    )(q, k, v)


def _layer1_out_kernel(o1_ref, o2_ref, o3_ref, l1_ref, l2_ref, l3_ref, gate_ref, h_ref, p_ref,
                       wout_ref, pg_ref, wpg_ref, wpp_ref, out_ref, vbuf):
    cols = []
    dils = [dil for _, dil in DILATION_GROUPS]
    for hg in range(N_HG):
        ls = [_load_view(ref, hg, dil, vbuf, 4 * g + 2)
              for g, (ref, dil) in enumerate(zip((l1_ref, l2_ref, l3_ref), dils))]
        os = [_load_view(ref, hg, dil, vbuf, 4 * g)
              for g, (ref, dil) in enumerate(zip((o1_ref, o2_ref, o3_ref), dils))]
        m = jnp.maximum(jnp.maximum(ls[0], ls[1]), ls[2])
        es = [jnp.exp(l - m) for l in ls]
        num = es[0] * os[0] + es[1] * os[1] + es[2] * os[2]
        cols.append(num / (es[0] + es[1] + es[2]))
    o = jnp.concatenate(cols, axis=1)
    y = jnp.dot((o * gate_ref[...].astype(F32)).astype(BF16), wout_ref[...],
                preferred_element_type=F32)
    hmid = h_ref[...] + y
    out_ref[...] = _ple(hmid, p_ref[...], pg_ref[...], wpg_ref[...], wpp_ref[...])


def _layer1_out(os, lses, gate, h, p, wout, pg, wpg, wpp):
    B, S, D = h.shape
    row = lambda b, t: (b, t, 0)
    views = [_view_spec(dil) for _, dil in DILATION_GROUPS]
    return pl.pallas_call(
        _layer1_out_kernel,
        grid=(B, S // TM),
        in_specs=views + views + [
            pl.BlockSpec((None, TM, ATTN_WIDTH), row),
            pl.BlockSpec((None, TM, D), row),
            pl.BlockSpec((None, None, TM, PLE_DIM), lambda b, t: (1, b, t, 0)),
            _const_spec(wout.shape), _const_spec(pg.shape), _const_spec(wpg.shape),
            _const_spec(wpp.shape)],
        out_specs=pl.BlockSpec((None, TM, D), row),
        out_shape=jax.ShapeDtypeStruct((B, S, D), F32),
        scratch_shapes=[pltpu.VMEM((4 * N_GROUPS, TM, LANES), F32)],
        compiler_params=pltpu.CompilerParams(
            dimension_semantics=("arbitrary", "arbitrary"), vmem_limit_bytes=VMEM_LIMIT),
        name="layer1_out",
    )(*os, *lses, gate, h, p, wout, pg, wpg, wpp)


def _rope_tables(seq):
    inv = ROPE_THETA ** (-jnp.arange(ROPE_HALF, dtype=F32) * (2.0 / ROPE_DIM))
    ang = jnp.arange(seq).astype(F32)[:, None] * inv[None, :]
    rest = LANES - ROPE_LANES
    cos = jnp.concatenate([jnp.tile(jnp.cos(ang), (1, HEAD_GROUP)), jnp.ones((seq, rest), F32)], axis=1)
    sin = jnp.concatenate([jnp.tile(jnp.sin(ang), (1, HEAD_GROUP)), jnp.zeros((seq, rest), F32)], axis=1)
    return cos, sin


def kernel(x, p, norm_g, w_in_a, conv_w, conv_b, ln_g, ln_b, w_out_a, kv_norm_g, w_kv, k_norm_g,
           w_in_b, q_norm_g, w_out_b, ple_norm_g, w_ple_gate, w_ple_proj):
    B, S, D = x.shape
    assert (S, D) == (SEQ, D_MODEL) and S % TM == 0
    cos, sin = _rope_tables(S)
    src = _qk_lane_source()
    head_of_lane = src // HEAD_DIM
    bd = jnp.asarray(head_of_lane[:, None] == head_of_lane[None, :], BF16)
    qk_cols = np.concatenate([hg * HG_W + src for hg in range(N_HG)])
    row = lambda a: a.reshape(1, -1)
    slabs = lambda a: a.reshape(N_SLAB, 1, LANES)
    cw = jnp.pad(conv_w[0], ((0, HALO - CONV_WIDTH), (0, 0)))
    cw = cw.reshape(HALO, N_SLAB, LANES).transpose(1, 0, 2)
    wkv = jnp.concatenate([w_kv[:, :ATTN_WIDTH][:, qk_cols], w_kv[:, ATTN_WIDTH:]], axis=1)

    win_a = w_in_a[0].astype(BF16)
    wab = win_a[:, :2 * CONV_CH].reshape(D, N_SLAB, 2 * LANES).transpose(1, 0, 2)
    h1, *kvs = _layer0(
        x, p, row(norm_g[0]), wab, win_a[:, 2 * CONV_CH:], cw, slabs(conv_b[0]), slabs(ln_g[0]),
        slabs(ln_b[0]), w_out_a[0].astype(BF16), row(ple_norm_g[0]), w_ple_gate[0].astype(BF16),
        w_ple_proj[0].astype(BF16), row(kv_norm_g), wkv.astype(BF16),
        row(k_norm_g[src % HEAD_DIM]), bd, cos, sin)

    win_b = jnp.concatenate(
        [w_in_b[0][:, g * ATTN_WIDTH:(g + 1) * ATTN_WIDTH][:, qk_cols] for g in range(N_GROUPS)]
        + [w_in_b[0][:, N_GROUPS * ATTN_WIDTH:]], axis=1)
    q1, q2, q3, gate = _qproj(h1, row(norm_g[1]), win_b.astype(BF16),
                              q_norm_g[0][:, src % HEAD_DIM], bd, cos, sin)
    os, lses = [], []
    for g, q in enumerate((q1, q2, q3)):
        o, lse = _attention(q, kvs[2 * g], kvs[2 * g + 1], DILATION_GROUPS[g][1])
        os.append(o)
        lses.append(lse)
    return _layer1_out(os, lses, gate, h1, p, w_out_b[0].astype(BF16), row(ple_norm_g[1]),
                       w_ple_gate[1].astype(BF16), w_ple_proj[1].astype(BF16))
```

```python
import functools
import math

import jax
import jax.numpy as jnp
import numpy as np
from jax import lax
from jax.experimental import pallas as pl
from jax.experimental.pallas import tpu as pltpu

D_MODEL = 1024
SEQ = 2048
PLE_DIM = 256
CONV_WIDTH = 31
CONV_CH = 2048
HEAD_DIM = 64
N_HEADS = 16
ATTN_WIDTH = 1024
DILATION_GROUPS = ((128, 1), (512, 4), (2048, 16))
N_GROUPS = 3
ROPE_THETA = 500000.0
ROPE_DIM = 16
EPS = 1e-6
NEG_INF = -1e30

LANES = 128
HEAD_GROUP = 4
HG_W = HEAD_GROUP * HEAD_DIM
N_HG = N_HEADS // HEAD_GROUP
SPAN = 128
TM = 256
HALO = 32
CONV_ROWS = 64
N_SLAB = CONV_CH // LANES
IN_CHUNK = 512
VMEM_LIMIT = 56 * 1024 * 1024

ROPE_HALF = ROPE_DIM // 2
ROPE_LANES = HEAD_GROUP * ROPE_HALF
PASS_LANES = (HEAD_DIM - ROPE_DIM) // 2
LOG2E = math.log2(math.e)
LN2 = math.log(2.0)

F32 = jnp.float32
BF16 = jnp.bfloat16


def _qk_lane_source():
    src = np.zeros(HG_W, np.int32)
    for lane in range(HG_W):
        half, la = divmod(lane, LANES)
        if la < ROPE_LANES:
            h, i = divmod(la, ROPE_HALF)
            d = i + ROPE_HALF * half
        else:
            h, pd = divmod(la - ROPE_LANES, PASS_LANES)
            d = ROPE_DIM + pd + PASS_LANES * half
        src[lane] = h * HEAD_DIM + d
    return src


def _qk_head_masks(shape):
    la = lax.broadcasted_iota(jnp.int32, shape, len(shape) - 1) & (LANES - 1)
    return [((la >= ROPE_HALF * h) & (la < ROPE_HALF * (h + 1)))
            | ((la >= ROPE_LANES + PASS_LANES * h) & (la < ROPE_LANES + PASS_LANES * (h + 1)))
            for h in range(HEAD_GROUP)]


def _const_spec(shape):
    nd = len(shape)
    return pl.BlockSpec(shape, lambda *_: (0,) * nd, pipeline_mode=pl.Buffered(1))


def _rmsnorm(x, g):
    ms = jnp.sum(x * x, axis=-1, keepdims=True) * (1.0 / x.shape[-1])
    return x * lax.rsqrt(ms + EPS) * g


def _sigmoid(x):
    return 0.5 * jnp.tanh(0.5 * x) + 0.5


def _silu(x):
    h = 0.5 * x
    return h * jnp.tanh(h) + h


def _head_norm_rope(x, bd, g, cos, sin):
    ss = jnp.dot((x * x).astype(BF16), bd, preferred_element_type=F32)
    xn = x * lax.rsqrt(ss * (1.0 / HEAD_DIM) + EPS) * g
    a, b = xn[:, :LANES], xn[:, LANES:]
    return jnp.concatenate([a * cos - b * sin, b * cos + a * sin], axis=1)


def _view_spec(dil):
    return pl.BlockSpec((None, N_HG, TM // dil, dil * HG_W), lambda b, t: (b, 0, t, 0))


def _view_shape(batch, dil, dtype):
    return jax.ShapeDtypeStruct((batch, N_HG, SEQ // dil, dil * HG_W), dtype)


def _stage_pitch(dil):
    return dil + 1 if dil % 8 == 0 else dil


def _stage_rows(dil):
    return (TM // dil) * _stage_pitch(dil)


def _store_views(x, hg, scrs, slab0, refs):
    halves = HG_W // LANES
    for dil, ref in refs.items():
        if dil == 1:
            ref[hg] = x.astype(BF16)
            continue
        scr, pitch = scrs[dil], _stage_pitch(dil)
        for s in range(halves):
            xs = x[:, s * LANES:(s + 1) * LANES]
            if pitch == dil:
                scr[slab0 + s] = xs
            else:
                for m in range(TM // dil):
                    scr[slab0 + s, pl.ds(m * pitch, dil), :] = xs[m * dil:(m + 1) * dil]
        for r in range(dil):
            for s in range(halves):
                piece = scr[slab0 + s, pl.ds(r, TM // dil, stride=pitch), :]
                ref[hg, :, r * HG_W + s * LANES:r * HG_W + (s + 1) * LANES] = piece.astype(BF16)


def _load_view(ref, hg, dil, scrs, slab0):
    if dil == 1:
        return ref[hg].astype(F32)
    halves = HG_W // LANES
    scr, pitch = scrs[dil], _stage_pitch(dil)
    for r in range(dil):
        for s in range(halves):
            piece = ref[hg, :, r * HG_W + s * LANES:r * HG_W + (s + 1) * LANES]
            scr[slab0 + s, pl.ds(r, TM // dil, stride=pitch), :] = piece.astype(F32)
    cols = []
    for s in range(halves):
        if pitch == dil:
            cols.append(scr[slab0 + s])
        else:
            cols.append(jnp.concatenate(
                [scr[slab0 + s, pl.ds(m * pitch, dil), :] for m in range(TM // dil)], axis=0))
    return jnp.concatenate(cols, axis=1)


def _ple(hmid, p, pg, wpg, wpp):
    gate = _sigmoid(jnp.dot(_rmsnorm(hmid, pg).astype(BF16), wpg, preferred_element_type=F32))
    pe = jnp.dot(p.astype(BF16), wpp, preferred_element_type=F32)
    return hmid + gate * pe


def _layer0_kernel(x_ref, p_ref, ng_ref, win_ref, cw_ref, cb_ref, lng_ref, lnb_ref, wout_ref,
                   pg_ref, wpg_ref, wpp_ref, kvg_ref, wkv_ref, kg_ref, bd_ref, cos_ref, sin_ref,
                   h_ref, k1_ref, v1_ref, k4_ref, v4_ref, k16_ref, v16_ref,
                   gbuf, ybuf, zbuf, mbuf, vbuf4, vbuf16):
    t = pl.program_id(1)
    vbufs = {4: vbuf4, 16: vbuf16}

    @pl.when(t == 0)
    def _():
        gbuf[:, 0:HALO, :] = jnp.zeros((N_SLAB, HALO, LANES), F32)

    x = x_ref[...]
    u = _rmsnorm(x, ng_ref[...]).astype(BF16)

    for c in range(CONV_CH // IN_CHUNK):
        lo = c * IN_CHUNK
        a = jnp.dot(u, win_ref[:, lo:lo + IN_CHUNK], preferred_element_type=F32)
        b = jnp.dot(u, win_ref[:, CONV_CH + lo:CONV_CH + lo + IN_CHUNK], preferred_element_type=F32)
        z = jnp.dot(u, win_ref[:, 2 * CONV_CH + lo:2 * CONV_CH + lo + IN_CHUNK],
                    preferred_element_type=F32)
        glu = a * _sigmoid(b)
        sz = _silu(z)
        for s in range(IN_CHUNK // LANES):
            slab = c * (IN_CHUNK // LANES) + s
            gbuf[slab, HALO:HALO + TM, :] = glu[:, s * LANES:(s + 1) * LANES]
            zbuf[slab] = sz[:, s * LANES:(s + 1) * LANES]

    tap0 = HALO - (CONV_WIDTH - 1)

    def conv_slab(s, carry):
        w = cw_ref[s]
        bias = cb_ref[s]
        for rb in range(TM // CONV_ROWS):
            acc = jnp.zeros((CONV_ROWS, LANES), F32)
            for k in range(CONV_WIDTH):
                acc = acc + gbuf[s, pl.ds(rb * CONV_ROWS + tap0 + k, CONV_ROWS), :] * w[k:k + 1, :]
            ybuf[s, pl.ds(rb * CONV_ROWS, CONV_ROWS), :] = acc + bias
        return carry

    lax.fori_loop(0, N_SLAB, conv_slab, 0)

    gbuf[:, 0:HALO, :] = gbuf[:, TM:TM + HALO, :]

    tot = ybuf[0]
    for s in range(1, N_SLAB):
        tot = tot + ybuf[s]
    mu = jnp.sum(tot, axis=-1, keepdims=True) * (1.0 / CONV_CH)
    d0 = ybuf[0] - mu
    vs = d0 * d0
    for s in range(1, N_SLAB):
        ds = ybuf[s] - mu
        vs = vs + ds * ds
    rstd = lax.rsqrt(jnp.sum(vs, axis=-1, keepdims=True) * (1.0 / CONV_CH) + EPS)
    for s in range(N_SLAB):
        yn = (ybuf[s] - mu) * rstd * lng_ref[s] + lnb_ref[s]
        mbuf[:, s * LANES:(s + 1) * LANES] = (_silu(yn) * zbuf[s]).astype(BF16)
    hmid = x + jnp.dot(mbuf[...], wout_ref[...], preferred_element_type=F32)

    h = _ple(hmid, p_ref[...], pg_ref[...], wpg_ref[...], wpp_ref[...])
    h_ref[...] = h

    kv = jnp.dot(_rmsnorm(h, kvg_ref[...]).astype(BF16), wkv_ref[...], preferred_element_type=F32)
    bd = bd_ref[...]
    cos, sin = cos_ref[...], sin_ref[...]
    for hg in range(N_HG):
        kh = kv[:, hg * HG_W:(hg + 1) * HG_W]
        kh = _head_norm_rope(kh, bd, kg_ref[...], cos, sin)
        vh = kv[:, ATTN_WIDTH + hg * HG_W:ATTN_WIDTH + (hg + 1) * HG_W]
        _store_views(kh, hg, vbufs, 4 * hg, {1: k1_ref, 4: k4_ref, 16: k16_ref})
        _store_views(vh, hg, vbufs, 4 * hg + 2, {1: v1_ref, 4: v4_ref, 16: v16_ref})


def _layer0(x, p, ng, win, cw, cb, lng, lnb, wout, pg, wpg, wpp, kvg, wkv, kg, bd, cos, sin):
    B, S, D = x.shape
    nt = S // TM
    row = lambda b, t: (b, t, 0)
    tab = pl.BlockSpec((TM, LANES), lambda b, t: (t, 0))
    dils = [dil for _, dil in DILATION_GROUPS for _ in range(2)]
    return pl.pallas_call(
        _layer0_kernel,
        grid=(B, nt),
        in_specs=[
            pl.BlockSpec((None, TM, D), row),
            pl.BlockSpec((None, None, TM, PLE_DIM), lambda b, t: (0, b, t, 0)),
            _const_spec(ng.shape), _const_spec(win.shape), _const_spec(cw.shape),
            _const_spec(cb.shape), _const_spec(lng.shape), _const_spec(lnb.shape),
            _const_spec(wout.shape), _const_spec(pg.shape), _const_spec(wpg.shape),
            _const_spec(wpp.shape), _const_spec(kvg.shape), _const_spec(wkv.shape),
            _const_spec(kg.shape), _const_spec(bd.shape),
            tab, tab,
        ],
        out_specs=[pl.BlockSpec((None, TM, D), row)] + [_view_spec(dil) for dil in dils],
        out_shape=[jax.ShapeDtypeStruct((B, S, D), F32)] + [_view_shape(B, dil, BF16) for dil in dils],
        scratch_shapes=[
            pltpu.VMEM((N_SLAB, HALO + TM, LANES), F32),
            pltpu.VMEM((N_SLAB, TM, LANES), F32),
            pltpu.VMEM((N_SLAB, TM, LANES), F32),
            pltpu.VMEM((TM, CONV_CH), BF16),
            pltpu.VMEM((4 * N_HG, _stage_rows(4), LANES), F32),
            pltpu.VMEM((4 * N_HG, _stage_rows(16), LANES), F32),
        ],
        compiler_params=pltpu.CompilerParams(
            dimension_semantics=("arbitrary", "arbitrary"), vmem_limit_bytes=VMEM_LIMIT),
        name="layer0",
    )(x, p, ng, win, cw, cb, lng, lnb, wout, pg, wpg, wpp, kvg, wkv, kg, bd, cos, sin)


def _qproj_kernel(h_ref, ng_ref, win_ref, qg_ref, bd_ref, cos_ref, sin_ref,
                  q1_ref, q2_ref, q3_ref, gate_ref, vbuf4, vbuf16):
    u = _rmsnorm(h_ref[...], ng_ref[...]).astype(BF16)
    bd = bd_ref[...]
    cos, sin = cos_ref[...], sin_ref[...]
    for g, q_ref in enumerate((q1_ref, q2_ref, q3_ref)):
        q = jnp.dot(u, win_ref[:, g * ATTN_WIDTH:(g + 1) * ATTN_WIDTH], preferred_element_type=F32)
        for hg in range(N_HG):
            qh = _head_norm_rope(q[:, hg * HG_W:(hg + 1) * HG_W], bd, qg_ref[g:g + 1, :], cos, sin)
            _store_views(qh, hg, {4: vbuf4, 16: vbuf16}, 2 * hg, {DILATION_GROUPS[g][1]: q_ref})
    z = jnp.dot(u, win_ref[:, N_GROUPS * ATTN_WIDTH:], preferred_element_type=F32)
    gate_ref[...] = _silu(z).astype(BF16)


def _qproj(h, ng, win, qg, bd, cos, sin):
    B, S, D = h.shape
    row = lambda b, t: (b, t, 0)
    tab = pl.BlockSpec((TM, LANES), lambda b, t: (t, 0))
    dils = [dil for _, dil in DILATION_GROUPS]
    return pl.pallas_call(
        _qproj_kernel,
        grid=(B, S // TM),
        in_specs=[pl.BlockSpec((None, TM, D), row), _const_spec(ng.shape), _const_spec(win.shape),
                  _const_spec(qg.shape), _const_spec(bd.shape), tab, tab],
        out_specs=[_view_spec(dil) for dil in dils] + [pl.BlockSpec((None, TM, ATTN_WIDTH), row)],
        out_shape=[_view_shape(B, dil, BF16) for dil in dils]
        + [jax.ShapeDtypeStruct((B, S, ATTN_WIDTH), BF16)],
        scratch_shapes=[pltpu.VMEM((2 * N_HG, _stage_rows(4), LANES), F32),
                        pltpu.VMEM((2 * N_HG, _stage_rows(16), LANES), F32)],
        compiler_params=pltpu.CompilerParams(
            dimension_semantics=("arbitrary", "arbitrary"), vmem_limit_bytes=VMEM_LIMIT),
        name="qproj",
    )(h, ng, win, qg, bd, cos, sin)


def _attn_kernel(q_ref, k_ref, v_ref, o_ref, lse_ref, *, dil, nb):
    qk_mask = _qk_head_masks((1, HG_W))
    lane = lax.broadcasted_iota(jnp.int32, (1, HG_W), 1)
    v_mask = [(lane >= h * HEAD_DIM) & (lane < (h + 1) * HEAD_DIM) for h in range(HEAD_GROUP)]
    row = lax.broadcasted_iota(jnp.int32, (SPAN, 2 * SPAN), 0)
    col = lax.broadcasted_iota(jnp.int32, (SPAN, 2 * SPAN), 1)
    band = (col >= row) & (col <= row + SPAN)
    tri = (lax.broadcasted_iota(jnp.int32, (SPAN, SPAN), 1)
           <= lax.broadcasted_iota(jnp.int32, (SPAN, SPAN), 0))

    def select_heads(vals):
        out = vals[HEAD_GROUP - 1]
        for h in range(HEAD_GROUP - 2, -1, -1):
            out = jnp.where(v_mask[h], vals[h], out)
        return out

    for r in range(dil):
        cs = slice(r * HG_W, (r + 1) * HG_W)
        for j in range(nb):
            rows = slice(j * SPAN, (j + 1) * SPAN)
            q = q_ref[rows, cs]
            if j == 0:
                kk, vv, valid = k_ref[rows, cs], v_ref[rows, cs], tri
            else:
                both = slice((j - 1) * SPAN, (j + 1) * SPAN)
                kk, vv, valid = k_ref[both, cs], v_ref[both, cs], band
            qs = jnp.concatenate(
                [jnp.where(qk_mask[h], q, jnp.zeros_like(q)) for h in range(HEAD_GROUP)], axis=0)
            s = lax.dot_general(qs, kk, (((1,), (1,)), ((), ())), preferred_element_type=F32)
            ps, ms, ls = [], [], []
            for h in range(HEAD_GROUP):
                sh = jnp.where(valid, s[h * SPAN:(h + 1) * SPAN], NEG_INF)
                m = jnp.max(sh, axis=-1, keepdims=True)
                p = jnp.exp2(sh - m)
                ls.append(jnp.sum(p, axis=-1, keepdims=True))
                ms.append(m)
                ps.append(p.astype(BF16))
            of = jnp.dot(jnp.concatenate(ps, axis=0), vv, preferred_element_type=F32)
            o = select_heads([of[h * SPAN:(h + 1) * SPAN] for h in range(HEAD_GROUP)])
            l = select_heads([jnp.broadcast_to(x, (SPAN, HG_W)) for x in ls])
            m = select_heads([jnp.broadcast_to(x, (SPAN, HG_W)) for x in ms])
            o_ref[rows, cs] = (o / l).astype(BF16)
            lse_ref[rows, cs] = m * LN2 + jnp.log(l)


def _attention(q, k, v, dil):
    B, nhg, L, _ = q.shape
    nb = L // SPAN
    spec = pl.BlockSpec((None, None, L, dil * HG_W), lambda b, g: (b, g, 0, 0))
    return pl.pallas_call(
        functools.partial(_attn_kernel, dil=dil, nb=nb),
        grid=(B, nhg),
        in_specs=[spec, spec, spec],
        out_specs=[spec, spec],
        out_shape=[jax.ShapeDtypeStruct((B, nhg, L, dil * HG_W), BF16),
                   jax.ShapeDtypeStruct((B, nhg, L, dil * HG_W), F32)],
        compiler_params=pltpu.CompilerParams(
            dimension_semantics=("arbitrary", "arbitrary"), vmem_limit_bytes=VMEM_LIMIT),
        name=f"attn_d{dil}",
    )(q, k, v)


def _layer1_out_kernel(o1_ref, o2_ref, o3_ref, l1_ref, l2_ref, l3_ref, gate_ref, h_ref, p_ref,
                       wout_ref, pg_ref, wpg_ref, wpp_ref, out_ref, vbuf4, vbuf16):
    cols = []
    vbufs = {4: vbuf4, 16: vbuf16}
    dils = [dil for _, dil in DILATION_GROUPS]
    for hg in range(N_HG):
        ls = [_load_view(ref, hg, dil, vbufs, 2)
              for g, (ref, dil) in enumerate(zip((l1_ref, l2_ref, l3_ref), dils))]
        os = [_load_view(ref, hg, dil, vbufs, 0)
              for g, (ref, dil) in enumerate(zip((o1_ref, o2_ref, o3_ref), dils))]
        m = jnp.maximum(jnp.maximum(ls[0], ls[1]), ls[2])
        es = [jnp.exp(l - m) for l in ls]
        num = es[0] * os[0] + es[1] * os[1] + es[2] * os[2]
        cols.append(num / (es[0] + es[1] + es[2]))
    o = jnp.concatenate(cols, axis=1)
    y = jnp.dot((o * gate_ref[...].astype(F32)).astype(BF16), wout_ref[...],
                preferred_element_type=F32)
    hmid = h_ref[...] + y
    out_ref[...] = _ple(hmid, p_ref[...], pg_ref[...], wpg_ref[...], wpp_ref[...])


def _layer1_out(os, lses, gate, h, p, wout, pg, wpg, wpp):
    B, S, D = h.shape
    row = lambda b, t: (b, t, 0)
    views = [_view_spec(dil) for _, dil in DILATION_GROUPS]
    return pl.pallas_call(
        _layer1_out_kernel,
        grid=(B, S // TM),
        in_specs=views + views + [
            pl.BlockSpec((None, TM, ATTN_WIDTH), row),
            pl.BlockSpec((None, TM, D), row),
            pl.BlockSpec((None, None, TM, PLE_DIM), lambda b, t: (1, b, t, 0)),
            _const_spec(wout.shape), _const_spec(pg.shape), _const_spec(wpg.shape),
            _const_spec(wpp.shape)],
        out_specs=pl.BlockSpec((None, TM, D), row),
        out_shape=jax.ShapeDtypeStruct((B, S, D), F32),
        scratch_shapes=[pltpu.VMEM((4, _stage_rows(4), LANES), F32),
                        pltpu.VMEM((4, _stage_rows(16), LANES), F32)],
        compiler_params=pltpu.CompilerParams(
            dimension_semantics=("arbitrary", "arbitrary"), vmem_limit_bytes=VMEM_LIMIT),
        name="layer1_out",
    )(*os, *lses, gate, h, p, wout, pg, wpg, wpp)


def _rope_tables(seq):
    inv = ROPE_THETA ** (-jnp.arange(ROPE_HALF, dtype=F32) * (2.0 / ROPE_DIM))
    ang = jnp.arange(seq).astype(F32)[:, None] * inv[None, :]
    rest = LANES - ROPE_LANES
    cos = jnp.concatenate([jnp.tile(jnp.cos(ang), (1, HEAD_GROUP)), jnp.ones((seq, rest), F32)], axis=1)
    sin = jnp.concatenate([jnp.tile(jnp.sin(ang), (1, HEAD_GROUP)), jnp.zeros((seq, rest), F32)], axis=1)
    return cos, sin


def kernel(x, p, norm_g, w_in_a, conv_w, conv_b, ln_g, ln_b, w_out_a, kv_norm_g, w_kv, k_norm_g,
           w_in_b, q_norm_g, w_out_b, ple_norm_g, w_ple_gate, w_ple_proj):
    B, S, D = x.shape
    assert (S, D) == (SEQ, D_MODEL) and S % TM == 0
    cos, sin = _rope_tables(S)
    src = _qk_lane_source()
    head_of_lane = src // HEAD_DIM
    bd = jnp.asarray(head_of_lane[:, None] == head_of_lane[None, :], BF16)
    qk_cols = np.concatenate([hg * HG_W + src for hg in range(N_HG)])
    row = lambda a: a.reshape(1, -1)
    slabs = lambda a: a.reshape(N_SLAB, 1, LANES)
    cw = jnp.pad(conv_w[0], ((0, HALO - CONV_WIDTH), (0, 0)))
    cw = cw.reshape(HALO, N_SLAB, LANES).transpose(1, 0, 2)
    wkv = jnp.concatenate([w_kv[:, :ATTN_WIDTH][:, qk_cols], w_kv[:, ATTN_WIDTH:]], axis=1)

    h1, *kvs = _layer0(
        x, p, row(norm_g[0]), w_in_a[0].astype(BF16), cw, slabs(conv_b[0]), slabs(ln_g[0]),
        slabs(ln_b[0]), w_out_a[0].astype(BF16), row(ple_norm_g[0]), w_ple_gate[0].astype(BF16),
        w_ple_proj[0].astype(BF16), row(kv_norm_g), wkv.astype(BF16),
        row(k_norm_g[src % HEAD_DIM]), bd, cos, sin)

    win_b = jnp.concatenate(
        [w_in_b[0][:, g * ATTN_WIDTH:(g + 1) * ATTN_WIDTH][:, qk_cols] for g in range(N_GROUPS)]
        + [w_in_b[0][:, N_GROUPS * ATTN_WIDTH:]], axis=1)
    q1, q2, q3, gate = _qproj(h1, row(norm_g[1]), win_b.astype(BF16),
                              q_norm_g[0][:, src % HEAD_DIM] * (HEAD_DIM ** -0.5 * LOG2E),
                              bd, cos, sin)
    os, lses = [], []
    for g, q in enumerate((q1, q2, q3)):
        o, lse = _attention(q, kvs[2 * g], kvs[2 * g + 1], DILATION_GROUPS[g][1])
        os.append(o)
        lses.append(lse)
    return _layer1_out(os, lses, gate, h1, p, w_out_b[0].astype(BF16), row(ple_norm_g[1]),
                       w_ple_gate[1].astype(BF16), w_ple_proj[1].astype(BF16))
```

```python
import functools
import math

import jax
import jax.numpy as jnp
import numpy as np
from jax import lax
from jax.experimental import pallas as pl
from jax.experimental.pallas import tpu as pltpu

D_MODEL = 1024
SEQ = 2048
PLE_DIM = 256
CONV_WIDTH = 31
CONV_CH = 2048
HEAD_DIM = 64
N_HEADS = 16
ATTN_WIDTH = 1024
DILATION_GROUPS = ((128, 1), (512, 4), (2048, 16))
N_GROUPS = 3
ROPE_THETA = 500000.0
ROPE_DIM = 16
EPS = 1e-6
NEG_INF = -1e30

LANES = 128
HEAD_GROUP = 4
HG_W = HEAD_GROUP * HEAD_DIM
N_HG = N_HEADS // HEAD_GROUP
SPAN = 128
TM = 256
HALO = 32
CONV_ROWS = 64
N_SLAB = CONV_CH // LANES
IN_CHUNK = 512
VMEM_LIMIT = 56 * 1024 * 1024

ROPE_HALF = ROPE_DIM // 2
ROPE_LANES = HEAD_GROUP * ROPE_HALF
PASS_LANES = (HEAD_DIM - ROPE_DIM) // 2
LSE_LANES = LANES // HEAD_GROUP
LOG2E = math.log2(math.e)
LN2 = math.log(2.0)

F32 = jnp.float32
BF16 = jnp.bfloat16


def _qk_lane_source():
    src = np.zeros(HG_W, np.int32)
    for lane in range(HG_W):
        half, la = divmod(lane, LANES)
        if la < ROPE_LANES:
            h, i = divmod(la, ROPE_HALF)
            d = i + ROPE_HALF * half
        else:
            h, pd = divmod(la - ROPE_LANES, PASS_LANES)
            d = ROPE_DIM + pd + PASS_LANES * half
        src[lane] = h * HEAD_DIM + d
    return src


def _qk_head_masks(shape):
    la = lax.broadcasted_iota(jnp.int32, shape, len(shape) - 1) & (LANES - 1)
    return [((la >= ROPE_HALF * h) & (la < ROPE_HALF * (h + 1)))
            | ((la >= ROPE_LANES + PASS_LANES * h) & (la < ROPE_LANES + PASS_LANES * (h + 1)))
            for h in range(HEAD_GROUP)]


def _const_spec(shape):
    nd = len(shape)
    return pl.BlockSpec(shape, lambda *_: (0,) * nd, pipeline_mode=pl.Buffered(1))


def _rmsnorm(x, g):
    ms = jnp.sum(x * x, axis=-1, keepdims=True) * (1.0 / x.shape[-1])
    return x * lax.rsqrt(ms + EPS) * g


def _sigmoid(x):
    return 0.5 * jnp.tanh(0.5 * x) + 0.5


def _silu(x):
    h = 0.5 * x
    return h * jnp.tanh(h) + h


def _head_norm_rope(x, bd, g, cos, sin):
    ss = jnp.dot((x * x).astype(BF16), bd, preferred_element_type=F32)
    xn = x * lax.rsqrt(ss * (1.0 / HEAD_DIM) + EPS) * g
    a, b = xn[:, :LANES], xn[:, LANES:]
    return jnp.concatenate([a * cos - b * sin, b * cos + a * sin], axis=1)


def _view_spec(dil):
    return pl.BlockSpec((None, N_HG, TM // dil, dil * HG_W), lambda b, t: (b, 0, t, 0))


def _view_shape(batch, dil, dtype):
    return jax.ShapeDtypeStruct((batch, N_HG, SEQ // dil, dil * HG_W), dtype)


def _stage_pitch(dil):
    return dil + 1 if dil % 8 == 0 else dil


def _stage_rows(dil):
    return (TM // dil) * _stage_pitch(dil)


def _store_views(x, hg, scrs, slab0, refs):
    halves = HG_W // LANES
    for dil, ref in refs.items():
        if dil == 1:
            ref[hg] = x.astype(BF16)
            continue
        scr, pitch = scrs[dil], _stage_pitch(dil)
        for s in range(halves):
            xs = x[:, s * LANES:(s + 1) * LANES]
            if pitch == dil:
                scr[slab0 + s] = xs
            else:
                for m in range(TM // dil):
                    scr[slab0 + s, pl.ds(m * pitch, dil), :] = xs[m * dil:(m + 1) * dil]
        for r in range(dil):
            for s in range(halves):
                piece = scr[slab0 + s, pl.ds(r, TM // dil, stride=pitch), :]
                ref[hg, :, r * HG_W + s * LANES:r * HG_W + (s + 1) * LANES] = piece.astype(BF16)


def _load_view(ref, hg, dil, scrs, slab0, width=HG_W):
    if dil == 1:
        return ref[hg].astype(F32)
    halves = width // LANES
    scr, pitch = scrs[dil], _stage_pitch(dil)
    for r in range(dil):
        for s in range(halves):
            piece = ref[hg, :, r * width + s * LANES:r * width + (s + 1) * LANES]
            scr[slab0 + s, pl.ds(r, TM // dil, stride=pitch), :] = piece.astype(F32)
    cols = []
    for s in range(halves):
        if pitch == dil:
            cols.append(scr[slab0 + s])
        else:
            cols.append(jnp.concatenate(
                [scr[slab0 + s, pl.ds(m * pitch, dil), :] for m in range(TM // dil)], axis=0))
    return jnp.concatenate(cols, axis=1)


def _ple(hmid, p, pg, wpg, wpp):
    gate = _sigmoid(jnp.dot(_rmsnorm(hmid, pg).astype(BF16), wpg, preferred_element_type=F32))
    pe = jnp.dot(p.astype(BF16), wpp, preferred_element_type=F32)
    return hmid + gate * pe


def _layer0_kernel(x_ref, p_ref, ng_ref, win_ref, cw_ref, cb_ref, lng_ref, lnb_ref, wout_ref,
                   pg_ref, wpg_ref, wpp_ref, kvg_ref, wkv_ref, kg_ref, bd_ref, cos_ref, sin_ref,
                   h_ref, k1_ref, v1_ref, k4_ref, v4_ref, k16_ref, v16_ref,
                   gbuf, ybuf, zbuf, mbuf, vbuf4, vbuf16):
    t = pl.program_id(1)
    vbufs = {4: vbuf4, 16: vbuf16}

    @pl.when(t == 0)
    def _():
        gbuf[:, 0:HALO, :] = jnp.zeros((N_SLAB, HALO, LANES), F32)

    x = x_ref[...]
    u = _rmsnorm(x, ng_ref[...]).astype(BF16)

    for c in range(CONV_CH // IN_CHUNK):
        lo = c * IN_CHUNK
        a = jnp.dot(u, win_ref[:, lo:lo + IN_CHUNK], preferred_element_type=F32)
        b = jnp.dot(u, win_ref[:, CONV_CH + lo:CONV_CH + lo + IN_CHUNK], preferred_element_type=F32)
        z = jnp.dot(u, win_ref[:, 2 * CONV_CH + lo:2 * CONV_CH + lo + IN_CHUNK],
                    preferred_element_type=F32)
        glu = a * _sigmoid(b)
        sz = _silu(z)
        for s in range(IN_CHUNK // LANES):
            slab = c * (IN_CHUNK // LANES) + s
            gbuf[slab, HALO:HALO + TM, :] = glu[:, s * LANES:(s + 1) * LANES]
            zbuf[slab] = sz[:, s * LANES:(s + 1) * LANES]

    tap0 = HALO - (CONV_WIDTH - 1)

    def conv_slab(s, carry):
        w = cw_ref[s]
        bias = cb_ref[s]
        for rb in range(TM // CONV_ROWS):
            acc = jnp.zeros((CONV_ROWS, LANES), F32)
            for k in range(CONV_WIDTH):
                acc = acc + gbuf[s, pl.ds(rb * CONV_ROWS + tap0 + k, CONV_ROWS), :] * w[k:k + 1, :]
            ybuf[s, pl.ds(rb * CONV_ROWS, CONV_ROWS), :] = acc + bias
        return carry

    lax.fori_loop(0, N_SLAB, conv_slab, 0)

    gbuf[:, 0:HALO, :] = gbuf[:, TM:TM + HALO, :]

    tot = ybuf[0]
    for s in range(1, N_SLAB):
        tot = tot + ybuf[s]
    mu = jnp.sum(tot, axis=-1, keepdims=True) * (1.0 / CONV_CH)
    d0 = ybuf[0] - mu
    vs = d0 * d0
    for s in range(1, N_SLAB):
        ds = ybuf[s] - mu
        vs = vs + ds * ds
    rstd = lax.rsqrt(jnp.sum(vs, axis=-1, keepdims=True) * (1.0 / CONV_CH) + EPS)
    for s in range(N_SLAB):
        yn = (ybuf[s] - mu) * rstd * lng_ref[s] + lnb_ref[s]
        mbuf[:, s * LANES:(s + 1) * LANES] = (_silu(yn) * zbuf[s]).astype(BF16)
    hmid = x + jnp.dot(mbuf[...], wout_ref[...], preferred_element_type=F32)

    h = _ple(hmid, p_ref[...], pg_ref[...], wpg_ref[...], wpp_ref[...])
    h_ref[...] = h

    kv = jnp.dot(_rmsnorm(h, kvg_ref[...]).astype(BF16), wkv_ref[...], preferred_element_type=F32)
    bd = bd_ref[...]
    cos, sin = cos_ref[...], sin_ref[...]
    for hg in range(N_HG):
        kh = kv[:, hg * HG_W:(hg + 1) * HG_W]
        kh = _head_norm_rope(kh, bd, kg_ref[...], cos, sin)
        vh = kv[:, ATTN_WIDTH + hg * HG_W:ATTN_WIDTH + (hg + 1) * HG_W]
        _store_views(kh, hg, vbufs, 4 * hg, {1: k1_ref, 4: k4_ref, 16: k16_ref})
        _store_views(vh, hg, vbufs, 4 * hg + 2, {1: v1_ref, 4: v4_ref, 16: v16_ref})


def _layer0(x, p, ng, win, cw, cb, lng, lnb, wout, pg, wpg, wpp, kvg, wkv, kg, bd, cos, sin):
    B, S, D = x.shape
    nt = S // TM
    row = lambda b, t: (b, t, 0)
    tab = pl.BlockSpec((TM, LANES), lambda b, t: (t, 0))
    dils = [dil for _, dil in DILATION_GROUPS for _ in range(2)]
    return pl.pallas_call(
        _layer0_kernel,
        grid=(B, nt),
        in_specs=[
            pl.BlockSpec((None, TM, D), row),
            pl.BlockSpec((None, None, TM, PLE_DIM), lambda b, t: (0, b, t, 0)),
            _const_spec(ng.shape), _const_spec(win.shape), _const_spec(cw.shape),
            _const_spec(cb.shape), _const_spec(lng.shape), _const_spec(lnb.shape),
            _const_spec(wout.shape), _const_spec(pg.shape), _const_spec(wpg.shape),
            _const_spec(wpp.shape), _const_spec(kvg.shape), _const_spec(wkv.shape),
            _const_spec(kg.shape), _const_spec(bd.shape),
            tab, tab,
        ],
        out_specs=[pl.BlockSpec((None, TM, D), row)] + [_view_spec(dil) for dil in dils],
        out_shape=[jax.ShapeDtypeStruct((B, S, D), F32)] + [_view_shape(B, dil, BF16) for dil in dils],
        scratch_shapes=[
            pltpu.VMEM((N_SLAB, HALO + TM, LANES), F32),
            pltpu.VMEM((N_SLAB, TM, LANES), F32),
            pltpu.VMEM((N_SLAB, TM, LANES), F32),
            pltpu.VMEM((TM, CONV_CH), BF16),
            pltpu.VMEM((4 * N_HG, _stage_rows(4), LANES), F32),
            pltpu.VMEM((4 * N_HG, _stage_rows(16), LANES), F32),
        ],
        compiler_params=pltpu.CompilerParams(
            dimension_semantics=("arbitrary", "arbitrary"), vmem_limit_bytes=VMEM_LIMIT),
        name="layer0",
    )(x, p, ng, win, cw, cb, lng, lnb, wout, pg, wpg, wpp, kvg, wkv, kg, bd, cos, sin)


def _qproj_kernel(h_ref, ng_ref, win_ref, qg_ref, bd_ref, cos_ref, sin_ref,
                  q1_ref, q2_ref, q3_ref, gate_ref, vbuf4, vbuf16):
    u = _rmsnorm(h_ref[...], ng_ref[...]).astype(BF16)
    bd = bd_ref[...]
    cos, sin = cos_ref[...], sin_ref[...]
    for g, q_ref in enumerate((q1_ref, q2_ref, q3_ref)):
        q = jnp.dot(u, win_ref[:, g * ATTN_WIDTH:(g + 1) * ATTN_WIDTH], preferred_element_type=F32)
        for hg in range(N_HG):
            qh = _head_norm_rope(q[:, hg * HG_W:(hg + 1) * HG_W], bd, qg_ref[g:g + 1, :], cos, sin)
            _store_views(qh, hg, {4: vbuf4, 16: vbuf16}, 2 * hg, {DILATION_GROUPS[g][1]: q_ref})
    z = jnp.dot(u, win_ref[:, N_GROUPS * ATTN_WIDTH:], preferred_element_type=F32)
    gate_ref[...] = _silu(z).astype(BF16)


def _qproj(h, ng, win, qg, bd, cos, sin):
    B, S, D = h.shape
    row = lambda b, t: (b, t, 0)
    tab = pl.BlockSpec((TM, LANES), lambda b, t: (t, 0))
    dils = [dil for _, dil in DILATION_GROUPS]
    return pl.pallas_call(
        _qproj_kernel,
        grid=(B, S // TM),
        in_specs=[pl.BlockSpec((None, TM, D), row), _const_spec(ng.shape), _const_spec(win.shape),
                  _const_spec(qg.shape), _const_spec(bd.shape), tab, tab],
        out_specs=[_view_spec(dil) for dil in dils] + [pl.BlockSpec((None, TM, ATTN_WIDTH), row)],
        out_shape=[_view_shape(B, dil, BF16) for dil in dils]
        + [jax.ShapeDtypeStruct((B, S, ATTN_WIDTH), BF16)],
        scratch_shapes=[pltpu.VMEM((2 * N_HG, _stage_rows(4), LANES), F32),
                        pltpu.VMEM((2 * N_HG, _stage_rows(16), LANES), F32)],
        compiler_params=pltpu.CompilerParams(
            dimension_semantics=("arbitrary", "arbitrary"), vmem_limit_bytes=VMEM_LIMIT),
        name="qproj",
    )(h, ng, win, qg, bd, cos, sin)


def _attn_kernel(q_ref, k_ref, v_ref, o_ref, lse_ref, *, dil, nb):
    qk_mask = _qk_head_masks((1, HG_W))
    lane = lax.broadcasted_iota(jnp.int32, (1, HG_W), 1)
    v_mask = [(lane >= h * HEAD_DIM) & (lane < (h + 1) * HEAD_DIM) for h in range(HEAD_GROUP)]
    row = lax.broadcasted_iota(jnp.int32, (SPAN, 2 * SPAN), 0)
    col = lax.broadcasted_iota(jnp.int32, (SPAN, 2 * SPAN), 1)
    band = (col >= row) & (col <= row + SPAN)
    tri = (lax.broadcasted_iota(jnp.int32, (SPAN, SPAN), 1)
           <= lax.broadcasted_iota(jnp.int32, (SPAN, SPAN), 0))

    lane128 = lax.broadcasted_iota(jnp.int32, (1, LANES), 1)
    lse_mask = [(lane128 >= h * LSE_LANES) & (lane128 < (h + 1) * LSE_LANES)
                for h in range(HEAD_GROUP)]

    def select_heads(vals, masks):
        out = vals[HEAD_GROUP - 1]
        for h in range(HEAD_GROUP - 2, -1, -1):
            out = jnp.where(masks[h], vals[h], out)
        return out

    for r in range(dil):
        cs = slice(r * HG_W, (r + 1) * HG_W)
        for j in range(nb):
            rows = slice(j * SPAN, (j + 1) * SPAN)
            q = q_ref[rows, cs]
            if j == 0:
                kk, vv, valid = k_ref[rows, cs], v_ref[rows, cs], tri
            else:
                both = slice((j - 1) * SPAN, (j + 1) * SPAN)
                kk, vv, valid = k_ref[both, cs], v_ref[both, cs], band
            qs = jnp.concatenate(
                [jnp.where(qk_mask[h], q, jnp.zeros_like(q)) for h in range(HEAD_GROUP)], axis=0)
            s = lax.dot_general(qs, kk, (((1,), (1,)), ((), ())), preferred_element_type=F32)
            ps, ms, ls = [], [], []
            for h in range(HEAD_GROUP):
                sh = jnp.where(valid, s[h * SPAN:(h + 1) * SPAN], NEG_INF)
                m = jnp.max(sh, axis=-1, keepdims=True)
                p = jnp.exp2(sh - m)
                ls.append(jnp.sum(p, axis=-1, keepdims=True))
                ms.append(m)
                ps.append(p.astype(BF16))
            of = jnp.dot(jnp.concatenate(ps, axis=0), vv, preferred_element_type=F32)
            o = select_heads([of[h * SPAN:(h + 1) * SPAN] for h in range(HEAD_GROUP)], v_mask)
            l = select_heads([jnp.broadcast_to(x, (SPAN, HG_W)) for x in ls], v_mask)
            o_ref[rows, cs] = (o / l).astype(BF16)
            m_c = select_heads([jnp.broadcast_to(x, (SPAN, LANES)) for x in ms], lse_mask)
            l_c = select_heads([jnp.broadcast_to(x, (SPAN, LANES)) for x in ls], lse_mask)
            lse_ref[rows, r * LANES:(r + 1) * LANES] = m_c * LN2 + jnp.log(l_c)


def _attention(q, k, v, dil):
    B, nhg, L, _ = q.shape
    nb = L // SPAN
    spec = pl.BlockSpec((None, None, L, dil * HG_W), lambda b, g: (b, g, 0, 0))
    lse_spec = pl.BlockSpec((None, None, L, dil * LANES), lambda b, g: (b, g, 0, 0))
    return pl.pallas_call(
        functools.partial(_attn_kernel, dil=dil, nb=nb),
        grid=(B, nhg),
        in_specs=[spec, spec, spec],
        out_specs=[spec, lse_spec],
        out_shape=[jax.ShapeDtypeStruct((B, nhg, L, dil * HG_W), BF16),
                   jax.ShapeDtypeStruct((B, nhg, L, dil * LANES), F32)],
        compiler_params=pltpu.CompilerParams(
            dimension_semantics=("arbitrary", "arbitrary"), vmem_limit_bytes=VMEM_LIMIT),
        name=f"attn_d{dil}",
    )(q, k, v)


def _expand_heads(w, ex):
    hi = w.astype(BF16)
    lo = (w - hi.astype(F32)).astype(BF16)
    return jnp.dot(jnp.concatenate([hi, lo], axis=1), ex, preferred_element_type=F32)


def _layer1_out_kernel(o1_ref, o2_ref, o3_ref, l1_ref, l2_ref, l3_ref, gate_ref, h_ref, p_ref,
                       wout_ref, pg_ref, wpg_ref, wpp_ref, ex_ref, out_ref, vbuf4, vbuf16):
    cols = []
    vbufs = {4: vbuf4, 16: vbuf16}
    dils = [dil for _, dil in DILATION_GROUPS]
    ex = ex_ref[...]
    for hg in range(N_HG):
        ls = [_load_view(ref, hg, dil, vbufs, 2, LANES)
              for ref, dil in zip((l1_ref, l2_ref, l3_ref), dils)]
        os = [_load_view(ref, hg, dil, vbufs, 0)
              for ref, dil in zip((o1_ref, o2_ref, o3_ref), dils)]
        m = jnp.maximum(jnp.maximum(ls[0], ls[1]), ls[2])
        es = [jnp.exp(l - m) for l in ls]
        inv = 1.0 / (es[0] + es[1] + es[2])
        ws = [_expand_heads(e * inv, ex) for e in es]
        cols.append(ws[0] * os[0] + ws[1] * os[1] + ws[2] * os[2])
    o = jnp.concatenate(cols, axis=1)
    y = jnp.dot((o * gate_ref[...].astype(F32)).astype(BF16), wout_ref[...],
                preferred_element_type=F32)
    hmid = h_ref[...] + y
    out_ref[...] = _ple(hmid, p_ref[...], pg_ref[...], wpg_ref[...], wpp_ref[...])


def _layer1_out(os, lses, gate, h, p, wout, pg, wpg, wpp, ex):
    B, S, D = h.shape
    row = lambda b, t: (b, t, 0)
    views = [_view_spec(dil) for _, dil in DILATION_GROUPS]
    lse_views = [pl.BlockSpec((None, N_HG, TM // dil, dil * LANES), lambda b, t: (b, 0, t, 0))
                 for _, dil in DILATION_GROUPS]
    return pl.pallas_call(
        _layer1_out_kernel,
        grid=(B, S // TM),
        in_specs=views + lse_views + [
            pl.BlockSpec((None, TM, ATTN_WIDTH), row),
            pl.BlockSpec((None, TM, D), row),
            pl.BlockSpec((None, None, TM, PLE_DIM), lambda b, t: (1, b, t, 0)),
            _const_spec(wout.shape), _const_spec(pg.shape), _const_spec(wpg.shape),
            _const_spec(wpp.shape), _const_spec(ex.shape)],
        out_specs=pl.BlockSpec((None, TM, D), row),
        out_shape=jax.ShapeDtypeStruct((B, S, D), F32),
        scratch_shapes=[pltpu.VMEM((4, _stage_rows(4), LANES), F32),
                        pltpu.VMEM((4, _stage_rows(16), LANES), F32)],
        compiler_params=pltpu.CompilerParams(
            dimension_semantics=("arbitrary", "arbitrary"), vmem_limit_bytes=VMEM_LIMIT),
        name="layer1_out",
    )(*os, *lses, gate, h, p, wout, pg, wpg, wpp, ex)


def _rope_tables(seq):
    inv = ROPE_THETA ** (-jnp.arange(ROPE_HALF, dtype=F32) * (2.0 / ROPE_DIM))
    ang = jnp.arange(seq).astype(F32)[:, None] * inv[None, :]
    rest = LANES - ROPE_LANES
    cos = jnp.concatenate([jnp.tile(jnp.cos(ang), (1, HEAD_GROUP)), jnp.ones((seq, rest), F32)], axis=1)
    sin = jnp.concatenate([jnp.tile(jnp.sin(ang), (1, HEAD_GROUP)), jnp.zeros((seq, rest), F32)], axis=1)
    return cos, sin


def kernel(x, p, norm_g, w_in_a, conv_w, conv_b, ln_g, ln_b, w_out_a, kv_norm_g, w_kv, k_norm_g,
           w_in_b, q_norm_g, w_out_b, ple_norm_g, w_ple_gate, w_ple_proj):
    B, S, D = x.shape
    assert (S, D) == (SEQ, D_MODEL) and S % TM == 0
    cos, sin = _rope_tables(S)
    src = _qk_lane_source()
    head_of_lane = src // HEAD_DIM
    bd = jnp.asarray(head_of_lane[:, None] == head_of_lane[None, :], BF16)
    qk_cols = np.concatenate([hg * HG_W + src for hg in range(N_HG)])
    row = lambda a: a.reshape(1, -1)
    slabs = lambda a: a.reshape(N_SLAB, 1, LANES)
    cw = jnp.pad(conv_w[0], ((0, HALO - CONV_WIDTH), (0, 0)))
    cw = cw.reshape(HALO, N_SLAB, LANES).transpose(1, 0, 2)
    wkv = jnp.concatenate([w_kv[:, :ATTN_WIDTH][:, qk_cols], w_kv[:, ATTN_WIDTH:]], axis=1)

    h1, *kvs = _layer0(
        x, p, row(norm_g[0]), w_in_a[0].astype(BF16), cw, slabs(conv_b[0]), slabs(ln_g[0]),
        slabs(ln_b[0]), w_out_a[0].astype(BF16), row(ple_norm_g[0]), w_ple_gate[0].astype(BF16),
        w_ple_proj[0].astype(BF16), row(kv_norm_g), wkv.astype(BF16),
        row(k_norm_g[src % HEAD_DIM]), bd, cos, sin)

    win_b = jnp.concatenate(
        [w_in_b[0][:, g * ATTN_WIDTH:(g + 1) * ATTN_WIDTH][:, qk_cols] for g in range(N_GROUPS)]
        + [w_in_b[0][:, N_GROUPS * ATTN_WIDTH:]], axis=1)
    q1, q2, q3, gate = _qproj(h1, row(norm_g[1]), win_b.astype(BF16),
                              q_norm_g[0][:, src % HEAD_DIM] * (HEAD_DIM ** -0.5 * LOG2E),
                              bd, cos, sin)
    os, lses = [], []
    for g, q in enumerate((q1, q2, q3)):
        o, lse = _attention(q, kvs[2 * g], kvs[2 * g + 1], DILATION_GROUPS[g][1])
        os.append(o)
        lses.append(lse)
    pick = np.arange(LANES)[:, None] == LSE_LANES * (np.arange(HG_W)[None, :] // HEAD_DIM)
    ex = jnp.asarray(np.tile(pick, (2, 1)), BF16)
    return _layer1_out(os, lses, gate, h1, p, w_out_b[0].astype(BF16), row(ple_norm_g[1]),
                       w_ple_gate[1].astype(BF16), w_ple_proj[1].astype(BF16), ex)
```
